```python
import math
import jax
import jax.numpy as jnp
from jax import lax
import numpy as np


D_MODEL = 1024
BATCH = 4
SEQ = 8192
DEPTH = 2

GRID_W = 64
CTX_LEN = 256
EPS = 1e-6
ROPE_BASE = 10000.0
N_MOD = 6
NA_HEADS = 8
NA_HEAD_DIM = D_MODEL // 16
NA_WIDTH = NA_HEADS * NA_HEAD_DIM
NA_WIN_H = 8
NA_WIN_W = 16
FNET_GROUPS = 4
FNET_GROUP_DIM = D_MODEL // 8
FNET_WIDTH = FNET_GROUPS * FNET_GROUP_DIM
EVEN_IN = 3 * NA_WIDTH + FNET_WIDTH
EVEN_MIX = NA_WIDTH + FNET_WIDTH
DIFF_HEADS = 8
DIFF_HEAD_DIM = D_MODEL // 16
DIFF_QK_WIDTH = DIFF_HEADS * 2 * DIFF_HEAD_DIM
DIFF_V_WIDTH = DIFF_HEADS * 2 * DIFF_HEAD_DIM
ODD_IN = 2 * DIFF_QK_WIDTH + DIFF_V_WIDTH
Q_BLOCK = 128
PEER_HEADS = 8
PEER_N_KEYS = 128
PEER_N_EXPERTS = PEER_N_KEYS * PEER_N_KEYS
PEER_TOPK = 16
PEER_KEY_HALF = D_MODEL // 8
PEER_CHUNK = 128

kernel_name = 'hybrid_natten_fnet_diffattn_peer_dit'


def rms_norm(x, g):
    xf = x.astype(jnp.float32)
    y = xf * lax.rsqrt(jnp.mean(xf * xf, axis=-1, keepdims=True) + EPS)
    return (y * g.astype(jnp.float32)).astype(x.dtype)


def modulate(x, g, shift, scale):
    return rms_norm(x, g) * (1 + scale) + shift


def split_heads(t, n_heads):
    b, n, w = t.shape
    return t.reshape(b, n, n_heads, w // n_heads).transpose(0, 2, 1, 3)


def merge_heads(t):
    b, h, n, dh = t.shape
    return t.transpose(0, 2, 1, 3).reshape(b, n, h * dh)


def diff_qk_heads(t):
    b, n, _ = t.shape
    return t.reshape(b, n, DIFF_HEADS, 2, DIFF_HEAD_DIM).transpose(0, 2, 3, 1, 4)


def axial_rope_tables(n_tokens):
    quarter = DIFF_HEAD_DIM // 4
    t = jnp.arange(n_tokens)
    row = (t // GRID_W).astype(jnp.float32)
    col = (t % GRID_W).astype(jnp.float32)
    inv = ROPE_BASE ** (-jnp.arange(quarter, dtype=jnp.float32) / quarter)
    ang = jnp.stack([row[:, None] * inv, col[:, None] * inv], axis=1)
    return jnp.cos(ang), jnp.sin(ang)


def apply_axial_rope(x, cos, sin):
    shp = x.shape
    xf = x.astype(jnp.float32).reshape(shp[:-1] + (2, 2, shp[-1] // 4))
    x1 = xf[..., 0, :]
    x2 = xf[..., 1, :]
    out = jnp.stack([x1 * cos - x2 * sin, x2 * cos + x1 * sin], axis=-2)
    return out.reshape(shp).astype(x.dtype)


def dense_attention(q, k, v):
    s = jnp.einsum('bhqd,bhkd->bhqk', q, k).astype(jnp.float32) * (q.shape[-1] ** -0.5)
    p = jax.nn.softmax(s, axis=-1).astype(v.dtype)
    return jnp.einsum('bhqk,bhkd->bhqd', p, v)


def neighbourhood_attention(q, k, v, k_ctx, v_ctx, rpb):
    b, h, s, dh = q.shape
    rows = s // GRID_W
    win_h = min(NA_WIN_H, rows)
    n_nb = win_h * NA_WIN_W
    scale = dh ** -0.5
    qg = q.reshape(b, h, rows, GRID_W, dh)
    kg = k.reshape(b, h, rows, GRID_W, dh)
    vg = v.reshape(b, h, rows, GRID_W, dh)
    col = np.arange(GRID_W)
    col_start = np.clip(col - NA_WIN_W // 2, 0, GRID_W - NA_WIN_W)
    col_idx = col_start[:, None] + np.arange(NA_WIN_W)[None, :]
    col_bias = rpb[:, :, col_idx - col[:, None] + NA_WIN_W - 1]

    def row_block(r):
        r_start = jnp.clip(r - win_h // 2, 0, rows - win_h)
        q_r = lax.dynamic_index_in_dim(qg, r, axis=2, keepdims=False)
        k_nb = lax.dynamic_slice_in_dim(kg, r_start, win_h, axis=2)[:, :, :, col_idx]
        v_nb = lax.dynamic_slice_in_dim(vg, r_start, win_h, axis=2)[:, :, :, col_idx]
        row_off = r_start + jnp.arange(win_h) - r + NA_WIN_H - 1
        bias = jnp.take(col_bias, row_off, axis=1).transpose(0, 2, 1, 3)
        s_nb = jnp.einsum('bhqd,bhiqkd->bhqik', q_r, k_nb).astype(jnp.float32) * scale + bias[None]
        s_cx = jnp.einsum('bhqd,bhcd->bhqc', q_r, k_ctx).astype(jnp.float32) * scale
        p = jax.nn.softmax(jnp.concatenate([s_nb.reshape(b, h, GRID_W, n_nb), s_cx], axis=-1), axis=-1).astype(v.dtype)
        p_nb = p[..., :n_nb].reshape(b, h, GRID_W, win_h, NA_WIN_W)
        return (jnp.einsum('bhqik,bhiqkd->bhqd', p_nb, v_nb)
                + jnp.einsum('bhqc,bhcd->bhqd', p[..., n_nb:], v_ctx))

    out = lax.map(row_block, jnp.arange(rows))
    return out.transpose(1, 2, 0, 3, 4).reshape(b, h, s, dh)


def fourier_mix(f):
    b, n, _ = f.shape
    fg = f.astype(jnp.float32).reshape(b, n, FNET_GROUPS, FNET_GROUP_DIM)
    y = jnp.fft.fft2(fg, axes=(1, 3), norm='ortho').real
    return y.reshape(b, n, FNET_WIDTH).astype(f.dtype)


def diff_attend(qb, k_all, v_all, lam):
    s = jnp.einsum('bhcqd,bhckd->bhcqk', qb, k_all).astype(jnp.float32) * (qb.shape[-1] ** -0.5)
    p = jax.nn.softmax(s, axis=-1)
    a = p[:, :, 0] - lam * p[:, :, 1]
    return jnp.einsum('bhqk,bhkd->bhqd', a.astype(v_all.dtype), v_all)


def even_mixer(hl, hc, w_in, w_out, rpb, ctx_out):
    cuts = [NA_WIDTH, 2 * NA_WIDTH, 3 * NA_WIDTH]
    ql, kl, vl, fl = jnp.split(hl @ w_in, cuts, axis=-1)
    if ctx_out:
        qc, kc, vc, fc = jnp.split(hc @ w_in, cuts, axis=-1)
    else:
        kc, vc = jnp.split(hc @ w_in[:, NA_WIDTH:3 * NA_WIDTH], 2, axis=-1)
    kc_h = split_heads(kc, NA_HEADS)
    vc_h = split_heads(vc, NA_HEADS)
    a_l = neighbourhood_attention(split_heads(ql, NA_HEADS), split_heads(kl, NA_HEADS),
                                  split_heads(vl, NA_HEADS), kc_h, vc_h, rpb)
    yl = jnp.concatenate([merge_heads(a_l), fourier_mix(fl)], axis=-1) @ w_out
    yc = None
    if ctx_out:
        a_c = dense_attention(split_heads(qc, NA_HEADS), kc_h, vc_h)
        yc = jnp.concatenate([merge_heads(a_c), fourier_mix(fc)], axis=-1) @ w_out
    return yl, yc


def odd_mixer(hl, hc, w_in, w_out, lq1, lk1, lq2, lk2, subln_g, lambda_init, cos, sin, ctx_out):
    b, s, _ = hl.shape
    ql, kl, vl = jnp.split(hl @ w_in, [DIFF_QK_WIDTH, 2 * DIFF_QK_WIDTH], axis=-1)
    kc, vc = jnp.split(hc @ w_in[:, DIFF_QK_WIDTH:], [DIFF_QK_WIDTH], axis=-1)
    f32 = jnp.float32
    lam = (jnp.exp(jnp.sum(lq1.astype(f32) * lk1.astype(f32)))
           - jnp.exp(jnp.sum(lq2.astype(f32) * lk2.astype(f32))) + lambda_init)
    ql = apply_axial_rope(diff_qk_heads(ql), cos, sin)
    kl = apply_axial_rope(diff_qk_heads(kl), cos, sin)
    kc = diff_qk_heads(kc)
    vc_h = split_heads(vc, DIFF_HEADS)
    k_all = jnp.concatenate([kl, kc], axis=3)
    v_all = jnp.concatenate([split_heads(vl, DIFF_HEADS), vc_h], axis=2)
    nblk = s // Q_BLOCK
    qb = ql.reshape(b, DIFF_HEADS, 2, nblk, Q_BLOCK, DIFF_HEAD_DIM).transpose(3, 0, 1, 2, 4, 5)
    o_l = lax.map(lambda q: diff_attend(q, k_all, v_all, lam), qb)
    o_l = o_l.transpose(1, 2, 0, 3, 4).reshape(b, DIFF_HEADS, s, 2 * DIFF_HEAD_DIM)

    def post(o):
        return merge_heads(rms_norm(o, subln_g) * (1.0 - lambda_init))

    yl = post(o_l) @ w_out
    yc = None
    if ctx_out:
        qc = diff_qk_heads(hc @ w_in[:, :DIFF_QK_WIDTH])
        yc = post(diff_attend(qc, kc, vc_h, lam)) @ w_out
    return yl, yc


def peer_ffn(h, w_q, sub_keys, u, v):
    b, n, d = h.shape
    tok = h.reshape((b * n) // PEER_CHUNK, PEER_CHUNK, d)

    def chunk(t):
        q = (t @ w_q).reshape(PEER_CHUNK, PEER_HEADS, 2, PEER_KEY_HALF)
        s = jnp.einsum('thpk,hpnk->thpn', q, sub_keys).astype(jnp.float32)
        s_top, i_top = lax.top_k(s, PEER_TOPK)
        cand_s = s_top[:, :, 0, :, None] + s_top[:, :, 1, None, :]
        cand_i = i_top[:, :, 0, :, None] * PEER_N_KEYS + i_top[:, :, 1, None, :]
        best_s, best_j = lax.top_k(cand_s.reshape(PEER_CHUNK, PEER_HEADS, PEER_TOPK * PEER_TOPK), PEER_TOPK)
        idx = jnp.take_along_axis(cand_i.reshape(PEER_CHUNK, PEER_HEADS, PEER_TOPK * PEER_TOPK), best_j, axis=-1)
        g = jax.nn.softmax(best_s, axis=-1)
        a = jnp.einsum('td,thkd->thk', t, u[idx])
        w = (jax.nn.gelu(a.astype(jnp.float32)) * g).astype(t.dtype)
        return jnp.einsum('thk,thkd->td', w, v[idx])

    return lax.map(chunk, tok).reshape(b, n, d)


def setup_inputs(seed: int = 0) -> dict:
    key = jax.random.key(seed)
    ks = jax.random.split(key, 24)
    d = D_MODEL
    n_even = (DEPTH + 1) // 2
    n_odd = DEPTH // 2

    def nrm(k, shape, scale):
        return jax.random.normal(k, shape, jnp.float32) * scale

    return {
        'x': nrm(ks[0], (BATCH, SEQ, d), 1.0),
        'c': nrm(ks[1], (BATCH, d), 1.0),
        'ctx': nrm(ks[2], (BATCH, CTX_LEN, d), 1.0),
        'c_ctx': nrm(ks[3], (d,), 1.0),
        'w_mod': nrm(ks[4], (DEPTH, d, N_MOD * d), 0.5 * d ** -0.5),
        'b_mod': nrm(ks[5], (DEPTH, N_MOD * d), 0.02),
        'norm_mix_g': 1.0 + nrm(ks[6], (DEPTH, d), 0.02),
        'norm_ffn_g': 1.0 + nrm(ks[7], (DEPTH, d), 0.02),
        'even_w_in': nrm(ks[8], (n_even, d, EVEN_IN), d ** -0.5),
        'even_w_out': nrm(ks[9], (n_even, EVEN_MIX, d), EVEN_MIX ** -0.5),
        'na_rpb': nrm(ks[10], (n_even, NA_HEADS, 2 * NA_WIN_H - 1, 2 * NA_WIN_W - 1), 0.1),
        'odd_w_in': nrm(ks[11], (n_odd, d, ODD_IN), d ** -0.5),
        'odd_w_out': nrm(ks[12], (n_odd, DIFF_V_WIDTH, d), DIFF_V_WIDTH ** -0.5),
        'diff_lambda_q1': nrm(ks[13], (n_odd, DIFF_HEAD_DIM), 0.1),
        'diff_lambda_k1': nrm(ks[14], (n_odd, DIFF_HEAD_DIM), 0.1),
        'diff_lambda_q2': nrm(ks[15], (n_odd, DIFF_HEAD_DIM), 0.1),
        'diff_lambda_k2': nrm(ks[16], (n_odd, DIFF_HEAD_DIM), 0.1),
        'diff_subln_g': 1.0 + nrm(ks[17], (n_odd, 2 * DIFF_HEAD_DIM), 0.02),
        'peer_w_q': nrm(ks[18], (DEPTH, d, PEER_HEADS * 2 * PEER_KEY_HALF), d ** -0.5),
        'peer_sub_keys': nrm(ks[19], (DEPTH, PEER_HEADS, 2, PEER_N_KEYS, PEER_KEY_HALF), PEER_KEY_HALF ** -0.5),
        'peer_u': nrm(ks[20], (DEPTH, PEER_N_EXPERTS, d), d ** -0.5),
        'peer_v': nrm(ks[21], (DEPTH, PEER_N_EXPERTS, d), 0.25),
        'final_norm_g': 1.0 + nrm(ks[22], (d,), 0.02),
    }


def reference(x, c, ctx, c_ctx, w_mod, b_mod, norm_mix_g, norm_ffn_g, even_w_in, even_w_out, na_rpb,
              odd_w_in, odd_w_out, diff_lambda_q1, diff_lambda_k1, diff_lambda_q2, diff_lambda_k2,
              diff_subln_g, peer_w_q, peer_sub_keys, peer_u, peer_v, final_norm_g):
    s = x.shape[1]
    cos, sin = axial_rope_tables(s)
    xl, xc = x, ctx
    for layer in range(DEPTH):
        last = layer == DEPTH - 1
        j = layer // 2
        mod_l = (jax.nn.silu(c) @ w_mod[layer] + b_mod[layer])[:, None, :]
        mod_c = (jax.nn.silu(c_ctx) @ w_mod[layer] + b_mod[layer])[None, None, :]
        sh_m, sc_m, g_m, sh_f, sc_f, g_f = jnp.split(mod_l, N_MOD, axis=-1)
        csh_m, csc_m, cg_m, csh_f, csc_f, cg_f = jnp.split(mod_c, N_MOD, axis=-1)
        hl = modulate(xl, norm_mix_g[layer], sh_m, sc_m)
        hc = modulate(xc, norm_mix_g[layer], csh_m, csc_m)
        if layer % 2 == 0:
            yl, yc = even_mixer(hl, hc, even_w_in[j], even_w_out[j], na_rpb[j], not last)
        else:
            lambda_init = 0.8 - 0.6 * math.exp(-0.3 * layer)
            yl, yc = odd_mixer(hl, hc, odd_w_in[j], odd_w_out[j], diff_lambda_q1[j], diff_lambda_k1[j],
                               diff_lambda_q2[j], diff_lambda_k2[j], diff_subln_g[j], lambda_init,
                               cos, sin, not last)
        xl = xl + g_m * yl
        hl = modulate(xl, norm_ffn_g[layer], sh_f, sc_f)
        xl = xl + g_f * peer_ffn(hl, peer_w_q[layer], peer_sub_keys[layer], peer_u[layer], peer_v[layer])
        if not last:
            xc = xc + cg_m * yc
            hc = modulate(xc, norm_ffn_g[layer], csh_f, csc_f)
            xc = xc + cg_f * peer_ffn(hc, peer_w_q[layer], peer_sub_keys[layer], peer_u[layer], peer_v[layer])
    return rms_norm(xl, final_norm_g)
```

```python
import functools
import math

import jax
import jax.numpy as jnp
import numpy as np
from jax import lax
from jax.experimental import pallas as pl
from jax.experimental.pallas import tpu as pltpu

F32 = jnp.float32
BF16 = jnp.bfloat16

D_MODEL = 1024
GRID_W = 64
EPS = 1e-6
ROPE_BASE = 10000.0
NA_HEADS = 8
NA_HEAD_DIM = 64
NA_WIDTH = 512
NA_WIN_H = 8
NA_WIN_W = 16
FNET_GROUP_DIM = 128
DIFF_HEADS = 8
DIFF_HEAD_DIM = 64
PEER_HEADS = 8
PEER_N_KEYS = 128
PEER_TOPK = 16
LANES = 128
SUBLANES = 8
NEG = -1e30
VMEM_LIMIT = 56 * 1024 * 1024


def _cparams(sem, vmem=VMEM_LIMIT):
    return pltpu.CompilerParams(dimension_semantics=sem, vmem_limit_bytes=vmem)


def _mm_body(*refs, has_norm, emit_h, rope_tiles, has_res):
    it = iter(refs)
    x_ref = next(it)
    if has_norm:
        g_ref, sh_ref, sc_ref = next(it), next(it), next(it)
    w_ref = next(it)
    if rope_tiles:
        cos_ref, sin_ref = next(it), next(it)
    if has_res:
        res_ref, gate_ref = next(it), next(it)
    o_ref = next(it)
    if emit_h:
        h_ref = next(it)
    if has_norm:
        xs_ref = next(it)
    j = pl.program_id(1)

    if has_norm:
        @pl.when(j == 0)
        def _():
            xf = x_ref[...]
            y = xf * lax.rsqrt(jnp.mean(xf * xf, axis=-1, keepdims=True) + EPS)
            y = y * g_ref[...] * (1.0 + sc_ref[0]) + sh_ref[0]
            xs_ref[...] = y.astype(BF16)
            if emit_h:
                h_ref[...] = y
        a = xs_ref[...]
    else:
        a = x_ref[...].astype(BF16)
    acc = jnp.dot(a, w_ref[...], preferred_element_type=F32)

    def finish(v):
        if has_res:
            v = res_ref[...] + gate_ref[0] * v
        o_ref[...] = v.astype(o_ref.dtype)

    if rope_tiles:
        @pl.when(j < rope_tiles)
        def _():
            tn = acc.shape[1]
            reps = tn // LANES
            c = jnp.concatenate([cos_ref[...]] * reps, axis=1)
            s = jnp.concatenate([sin_ref[...]] * reps, axis=1)
            lane = lax.broadcasted_iota(jnp.int32, acc.shape, 1)
            first = (lane % 32) < 16
            partner = jnp.where(first, pltpu.roll(acc, tn - 16, 1), pltpu.roll(acc, 16, 1))
            finish(acc * c + partner * s)

        @pl.when(j >= rope_tiles)
        def _():
            finish(acc)
    else:
        finish(acc)


def _seg_of_block(i, tm, seg_rows, n_seg):
    return jnp.minimum((i * tm) // seg_rows, n_seg - 1)


def fused_matmul(x, w, *, norm=None, rope=None, res=None, emit_h=False, out_dtype=F32,
                 tm=512, tn=512, seg_rows=8192):
    m, k = x.shape
    n = w.shape[1]
    tm = min(tm, m)
    tn = min(tn, n)
    assert m % tm == 0 and n % tn == 0
    grid = (m // tm, n // tn)
    in_specs = [pl.BlockSpec((tm, k), lambda i, j: (i, 0))]
    args = [x]
    n_seg = 1
    if norm is not None:
        g, sh, sc = norm
        n_seg = sh.shape[0]
        seg = functools.partial(_seg_of_block, tm=tm, seg_rows=seg_rows, n_seg=n_seg)
        in_specs += [pl.BlockSpec((1, k), lambda i, j: (0, 0)),
                     pl.BlockSpec((1, 1, k), lambda i, j: (seg(i), 0, 0)),
                     pl.BlockSpec((1, 1, k), lambda i, j: (seg(i), 0, 0))]
        args += [g, sh, sc]
    in_specs.append(pl.BlockSpec((k, tn), lambda i, j: (0, j)))
    args.append(w)
    rope_tiles = 0
    if rope is not None:
        cos, sin, n_rope_cols, n_pos_blocks, n_lat_blocks = rope
        rope_tiles = n_rope_cols // tn
        pos = lambda i, j: (jnp.where(i < n_lat_blocks, i % n_pos_blocks, n_pos_blocks), 0)
        in_specs += [pl.BlockSpec((tm, LANES), pos), pl.BlockSpec((tm, LANES), pos)]
        args += [cos, sin]
    if res is not None:
        r, gate = res
        n_seg_r = gate.shape[0]
        segr = functools.partial(_seg_of_block, tm=tm, seg_rows=seg_rows, n_seg=n_seg_r)
        in_specs += [pl.BlockSpec((tm, tn), lambda i, j: (i, j)),
                     pl.BlockSpec((1, 1, tn), lambda i, j: (segr(i), 0, j))]
        args += [r, gate]
    out_shape = [jax.ShapeDtypeStruct((m, n), out_dtype)]
    out_specs = [pl.BlockSpec((tm, tn), lambda i, j: (i, j))]
    if emit_h:
        out_shape.append(jax.ShapeDtypeStruct((m, k), F32))
        out_specs.append(pl.BlockSpec((tm, k), lambda i, j: (i, 0)))
    scratch = [pltpu.VMEM((tm, k), BF16)] if norm is not None else []
    body = functools.partial(_mm_body, has_norm=norm is not None, emit_h=emit_h,
                             rope_tiles=rope_tiles, has_res=res is not None)
    outs = pl.pallas_call(
        body, grid=grid, in_specs=in_specs, out_specs=out_specs, out_shape=out_shape,
        scratch_shapes=scratch, compiler_params=_cparams(("parallel", "arbitrary")),
        name="fused_matmul")(*args)
    return outs if emit_h else outs[0]


def _na_body(q_ref, k_ref, v_ref, kc_ref, vc_ref, bias_ref, o_ref, *, rows):
    lane = lax.broadcasted_iota(jnp.int32, (GRID_W, LANES), 1)
    head_masks = [lane < NA_HEAD_DIM, lane >= NA_HEAD_DIM]
    kc = kc_ref[...]
    vc = vc_ref[...]
    scale = NA_HEAD_DIM ** -0.5
    nt = (((1,), (1,)), ((), ()))

    def row_body(r, carry):
        r_start = jnp.clip(r - NA_WIN_H // 2, 0, rows - NA_WIN_H)
        off = r_start - r + NA_WIN_H - 1
        q = q_ref[pl.ds(pl.multiple_of(r * GRID_W, GRID_W), GRID_W), :]
        k_start = pl.multiple_of(r_start * GRID_W, GRID_W)
        kw = k_ref[pl.ds(k_start, NA_WIN_H * GRID_W), :]
        vw = v_ref[pl.ds(k_start, NA_WIN_H * GRID_W), :]
        outs = []
        for hh in range(2):
            qm = jnp.where(head_masks[hh], q, jnp.zeros_like(q))
            s_nb = lax.dot_general(qm, kw, nt, preferred_element_type=F32) * scale + bias_ref[off, hh]
            s_cx = lax.dot_general(qm, kc, nt, preferred_element_type=F32) * scale
            m = jnp.maximum(jnp.max(s_nb, axis=1, keepdims=True), jnp.max(s_cx, axis=1, keepdims=True))
            p_nb = jnp.exp(s_nb - m)
            p_cx = jnp.exp(s_cx - m)
            l = jnp.sum(p_nb, axis=1, keepdims=True) + jnp.sum(p_cx, axis=1, keepdims=True)
            o = (jnp.dot(p_nb.astype(BF16), vw, preferred_element_type=F32)
                 + jnp.dot(p_cx.astype(BF16), vc, preferred_element_type=F32))
            outs.append(o / l)
        o_ref[pl.ds(pl.multiple_of(r * GRID_W, GRID_W), GRID_W), :] = jnp.where(
            head_masks[0], outs[0], outs[1]).astype(o_ref.dtype)
        return carry

    lax.fori_loop(0, rows, row_body, 0)


def _na_bias_table(rpb):
    col = np.arange(GRID_W)
    col_start = np.clip(col - NA_WIN_W // 2, 0, GRID_W - NA_WIN_W)
    kc = np.arange(GRID_W)
    inside = (kc[None, :] >= col_start[:, None]) & (kc[None, :] < col_start[:, None] + NA_WIN_W)
    dc = np.clip(kc[None, :] - col[:, None] + NA_WIN_W - 1, 0, 2 * NA_WIN_W - 2)
    per_dr = jnp.where(inside[None, None], rpb[:, :, dc], NEG)
    tabs = []
    for off in range(NA_WIN_H):
        blk = per_dr[:, off:off + NA_WIN_H]
        tabs.append(blk.transpose(0, 2, 1, 3).reshape(NA_HEADS, GRID_W, NA_WIN_H * GRID_W))
    return jnp.stack(tabs)


def neighbourhood_attention(qkv, rpb, *, batch, seq, ctx_len):
    rows = seq // GRID_W
    bias = _na_bias_table(rpb)
    hp = NA_HEADS // 2
    ctx_blk0 = (batch * seq) // ctx_len
    body = functools.partial(_na_body, rows=rows)
    return pl.pallas_call(
        body, grid=(batch, hp),
        in_specs=[pl.BlockSpec((seq, LANES), lambda b, h: (b, h)),
                  pl.BlockSpec((seq, LANES), lambda b, h: (b, hp + h)),
                  pl.BlockSpec((seq, LANES), lambda b, h: (b, 2 * hp + h)),
                  pl.BlockSpec((ctx_len, LANES), lambda b, h: (ctx_blk0 + b, hp + h)),
                  pl.BlockSpec((ctx_len, LANES), lambda b, h: (ctx_blk0 + b, 2 * hp + h)),
                  pl.BlockSpec((NA_WIN_H, 2, GRID_W, NA_WIN_H * GRID_W), lambda b, h: (0, h, 0, 0))],
        out_specs=pl.BlockSpec((seq, LANES), lambda b, h: (b, h)),
        out_shape=jax.ShapeDtypeStruct((batch * seq, NA_WIDTH), BF16),
        compiler_params=_cparams(("parallel", "parallel")), name="neighbourhood_attention",
    )(qkv, qkv, qkv, qkv, qkv, bias)


def _ctx_attn_body(q_ref, k_ref, v_ref, o_ref):
    lane = lax.broadcasted_iota(jnp.int32, q_ref.shape, 1)
    q = q_ref[...]
    k = k_ref[...]
    v = v_ref[...]
    scale = NA_HEAD_DIM ** -0.5
    outs = []
    for hh in range(2):
        msk = (lane < NA_HEAD_DIM) if hh == 0 else (lane >= NA_HEAD_DIM)
        qm = jnp.where(msk, q, jnp.zeros_like(q))
        s = lax.dot_general(qm, k, (((1,), (1,)), ((), ())), preferred_element_type=F32) * scale
        m = jnp.max(s, axis=1, keepdims=True)
        p = jnp.exp(s - m)
        l = jnp.sum(p, axis=1, keepdims=True)
        outs.append(jnp.dot(p.astype(BF16), v, preferred_element_type=F32) / l)
    o_ref[...] = jnp.where(lane < NA_HEAD_DIM, outs[0], outs[1]).astype(o_ref.dtype)


def context_attention(qkv, *, batch, seq, ctx_len):
    hp = NA_HEADS // 2
    ctx_blk0 = (batch * seq) // ctx_len
    return pl.pallas_call(
        _ctx_attn_body, grid=(batch, hp),
        in_specs=[pl.BlockSpec((ctx_len, LANES), lambda b, h: (ctx_blk0 + b, h)),
                  pl.BlockSpec((ctx_len, LANES), lambda b, h: (ctx_blk0 + b, hp + h)),
                  pl.BlockSpec((ctx_len, LANES), lambda b, h: (ctx_blk0 + b, 2 * hp + h))],
        out_specs=pl.BlockSpec((ctx_len, LANES), lambda b, h: (b, h)),
        out_shape=jax.ShapeDtypeStruct((batch * ctx_len, NA_WIDTH), BF16),
        compiler_params=_cparams(("parallel", "parallel")), name="context_attention",
    )(qkv, qkv, qkv)


def _dft_cs(n):
    k = np.arange(n)
    ang = 2.0 * np.pi * ((k[:, None] * k[None, :]) % n) / n
    return np.cos(ang), np.sin(ang)


def _fourier_long_body(f_ref, wch_ref, ma_ref, mb_ref, o_ref, pr_ref, pi_ref, z_ref, *, n1, n2):
    c = FNET_GROUP_DIM
    p = jnp.dot(f_ref[...].astype(BF16), wch_ref[...], preferred_element_type=F32)
    pr_ref[...] = p[:, :c]
    pi_ref[...] = p[:, c:]

    def stage_a(j, carry):
        rows = pl.ds(j, n1, stride=n2)
        x = jnp.concatenate([pr_ref[rows, :], pi_ref[rows, :]], axis=1).astype(BF16)
        y = jnp.dot(ma_ref[j], x, preferred_element_type=F32)
        zr = y[:n1, :c] - y[n1:, c:]
        zi = y[n1:, :c] + y[:n1, c:]
        z_ref[pl.ds(j, n1, stride=2 * n2), :] = zr
        z_ref[pl.ds(n2 + j, n1, stride=2 * n2), :] = zi
        return carry

    lax.fori_loop(0, n2, stage_a, 0)

    def stage_b(k1, carry):
        z = z_ref[pl.ds(pl.multiple_of(k1 * 2 * n2, 2 * n2), 2 * n2), :].astype(BF16)
        y = jnp.dot(mb_ref[...], z, preferred_element_type=F32)
        o_ref[pl.ds(k1, n2, stride=n1), :] = y.astype(o_ref.dtype)
        return carry

    lax.fori_loop(0, n1, stage_b, 0)


def fourier_long(f, *, batch, seq):
    c = FNET_GROUP_DIM
    groups = f.shape[1] // c
    n2 = 128
    n1 = seq // n2
    norm = 1.0 / math.sqrt(seq * c)
    cc, sc = _dft_cs(c)
    wch = np.concatenate([cc, -sc], axis=1)
    c1, s1 = _dft_cs(n1)
    j = np.arange(n2)
    k1 = np.arange(n1)
    tw = 2.0 * np.pi * (j[:, None] * k1[None, :]) / seq
    ar = np.cos(tw)[:, :, None] * c1[None] - np.sin(tw)[:, :, None] * s1[None]
    ai = -(np.cos(tw)[:, :, None] * s1[None] + np.sin(tw)[:, :, None] * c1[None])
    ma = np.concatenate([ar, ai], axis=1)
    c2, s2 = _dft_cs(n2)
    mb = np.concatenate([c2, s2], axis=1) * norm
    body = functools.partial(_fourier_long_body, n1=n1, n2=n2)
    return pl.pallas_call(
        body, grid=(batch, groups),
        in_specs=[pl.BlockSpec((seq, c), lambda b, g: (b, g)),
                  pl.BlockSpec((c, 2 * c), lambda b, g: (0, 0)),
                  pl.BlockSpec((n2, 2 * n1, n1), lambda b, g: (0, 0, 0)),
                  pl.BlockSpec((n2, 2 * n2), lambda b, g: (0, 0))],
        out_specs=pl.BlockSpec((seq, c), lambda b, g: (b, g)),
        out_shape=jax.ShapeDtypeStruct((batch * seq, groups * c), F32),
        scratch_shapes=[pltpu.VMEM((seq, c), F32), pltpu.VMEM((seq, c), F32), pltpu.VMEM((2 * seq, c), F32)],
        compiler_params=_cparams(("parallel", "parallel")), name="fourier_long",
    )(f, jnp.asarray(wch, BF16), jnp.asarray(ma, BF16), jnp.asarray(mb, BF16))


def _fourier_short_body(f_ref, wch_ref, mp_ref, o_ref):
    c = FNET_GROUP_DIM
    p = jnp.dot(f_ref[...].astype(BF16), wch_ref[...], preferred_element_type=F32)
    z = jnp.concatenate([p[:, :c], p[:, c:]], axis=0).astype(BF16)
    o_ref[...] = jnp.dot(mp_ref[...], z, preferred_element_type=F32).astype(o_ref.dtype)


def fourier_short(f, *, batch, n, row_block0):
    c = FNET_GROUP_DIM
    groups = f.shape[1] // c
    norm = 1.0 / math.sqrt(n * c)
    cc, sc = _dft_cs(c)
    wch = np.concatenate([cc, -sc], axis=1)
    cn, sn = _dft_cs(n)
    mp = np.concatenate([cn, sn], axis=1) * norm
    return pl.pallas_call(
        _fourier_short_body, grid=(batch, groups),
        in_specs=[pl.BlockSpec((n, c), lambda b, g: (row_block0 + b, g)),
                  pl.BlockSpec((c, 2 * c), lambda b, g: (0, 0)),
                  pl.BlockSpec((n, 2 * n), lambda b, g: (0, 0))],
        out_specs=pl.BlockSpec((n, c), lambda b, g: (b, g)),
        out_shape=jax.ShapeDtypeStruct((batch * n, groups * c), F32),
        compiler_params=_cparams(("parallel", "parallel")), name="fourier_short",
    )(f, jnp.asarray(wch, BF16), jnp.asarray(mp, BF16))


def _diff_body(lam_ref, q_ref, kl_ref, vl_ref, kc_ref, vc_ref, g_ref, o_ref,
               m_ref, l_ref, acc_ref, *, nk, out_scale):
    j = pl.program_id(3)
    d = DIFF_HEAD_DIM

    @pl.when(j == 0)
    def _():
        m_ref[...] = jnp.full(m_ref.shape, NEG, F32)
        l_ref[...] = jnp.zeros(l_ref.shape, F32)
        acc_ref[...] = jnp.zeros(acc_ref.shape, F32)

    q = q_ref[...] * (d ** -0.5)
    lane = lax.broadcasted_iota(jnp.int32, q.shape, 1)

    def step(k, v):
        for comp in range(2):
            msk = (lane < d) if comp == 0 else (lane >= d)
            qm = jnp.where(msk, q, jnp.zeros_like(q))
            s = lax.dot_general(qm, k, (((1,), (1,)), ((), ())), preferred_element_type=F32)
            m_old = m_ref[comp]
            m_new = jnp.maximum(m_old, jnp.max(s, axis=1, keepdims=True))
            alpha = jnp.exp(m_old - m_new)
            p = jnp.exp(s - m_new)
            l_ref[comp] = alpha * l_ref[comp] + jnp.sum(p, axis=1, keepdims=True)
            acc_ref[comp] = alpha * acc_ref[comp] + jnp.dot(p.astype(BF16), v, preferred_element_type=F32)
            m_ref[comp] = m_new

    @pl.when(j < nk)
    def _():
        step(kl_ref[...], vl_ref[...])

    @pl.when(j == nk)
    def _():
        step(kc_ref[...], vc_ref[...])
        lam = lam_ref[0]
        o = acc_ref[0] / l_ref[0] - lam * (acc_ref[1] / l_ref[1])
        y = o * lax.rsqrt(jnp.mean(o * o, axis=-1, keepdims=True) + EPS)
        o_ref[...] = (y * g_ref[...] * out_scale).astype(o_ref.dtype)


def diff_attention(qkv, lam, subln_g, lambda_init, *, batch, seq, ctx_len, tq=512, tk=512):
    h = DIFF_HEADS
    nq = seq // tq
    nk = seq // tk
    ctx_blk0 = (batch * seq) // ctx_len
    body = functools.partial(_diff_body, nk=nk, out_scale=1.0 - lambda_init)
    kv_idx = lambda off: (lambda b, hh, i, j: (b * nk + jnp.minimum(j, nk - 1), off + hh))
    cx_idx = lambda off: (lambda b, hh, i, j: (ctx_blk0 + b, off + hh))
    return pl.pallas_call(
        body, grid=(batch, h, nq, nk + 1),
        in_specs=[pl.BlockSpec(memory_space=pltpu.SMEM),
                  pl.BlockSpec((tq, LANES), lambda b, hh, i, j: (b * nq + i, hh)),
                  pl.BlockSpec((tk, LANES), kv_idx(h)),
                  pl.BlockSpec((tk, LANES), kv_idx(2 * h)),
                  pl.BlockSpec((ctx_len, LANES), cx_idx(h)),
                  pl.BlockSpec((ctx_len, LANES), cx_idx(2 * h)),
                  pl.BlockSpec((1, LANES), lambda b, hh, i, j: (0, 0))],
        out_specs=pl.BlockSpec((tq, LANES), lambda b, hh, i, j: (b * nq + i, hh)),
        out_shape=jax.ShapeDtypeStruct((batch * seq, h * LANES), BF16),
        scratch_shapes=[pltpu.VMEM((2, tq, 1), F32), pltpu.VMEM((2, tq, 1), F32),
                        pltpu.VMEM((2, tq, LANES), F32)],
        compiler_params=_cparams(("parallel", "parallel", "parallel", "arbitrary")),
        name="diff_attention",
    )(lam, qkv, qkv, qkv, qkv, qkv, subln_g)


def _top16(s, idx_vals):
    n = s.shape[0]
    pos = lax.broadcasted_iota(jnp.int32, s.shape, 0)
    vals, payload = [], []
    for _ in range(PEER_TOPK):
        m = jnp.max(s, axis=0, keepdims=True)
        sel = jnp.min(jnp.where(s == m, pos, n), axis=0, keepdims=True)
        hit = pos == sel
        vals.append(m)
        payload.append(jnp.max(jnp.where(hit, idx_vals, -1), axis=0, keepdims=True))
        s = jnp.where(hit, -jnp.inf, s)
    return jnp.concatenate(vals, axis=0), jnp.concatenate(payload, axis=0)


def _peer_topk_body(q_ref, keys_ref, idx_ref, gate_ref):
    q = q_ref[...]
    kd = q.shape[1] // 2
    tops = []
    for half in range(2):
        s = lax.dot_general(keys_ref[0, half], q[:, half * kd:(half + 1) * kd],
                            (((1,), (1,)), ((), ())), preferred_element_type=F32)
        key_idx = lax.broadcasted_iota(jnp.int32, s.shape, 0)
        tops.append(_top16(s, key_idx))
    (s0, i0), (s1, i1) = tops
    cand_s = jnp.concatenate([s0[a:a + 1] + s1 for a in range(PEER_TOPK)], axis=0)
    cand_i = jnp.concatenate([i0[a:a + 1] * PEER_N_KEYS + i1 for a in range(PEER_TOPK)], axis=0)
    best_s, best_i = _top16(cand_s, cand_i)
    e = jnp.exp(best_s - best_s[0:1])
    gate_ref[0] = e / jnp.sum(e, axis=0, keepdims=True)
    idx_ref[0] = best_i


def peer_topk(q, sub_keys, *, tb=256):
    t = q.shape[0]
    h = PEER_HEADS
    idx, gate = pl.pallas_call(
        _peer_topk_body, grid=(t // tb, h),
        in_specs=[pl.BlockSpec((tb, 2 * LANES), lambda i, hh: (i, hh)),
                  pl.BlockSpec((1, 2, PEER_N_KEYS, LANES), lambda i, hh: (hh, 0, 0, 0))],
        out_specs=[pl.BlockSpec((1, PEER_TOPK, tb), lambda i, hh: (hh, 0, i)),
                   pl.BlockSpec((1, PEER_TOPK, tb), lambda i, hh: (hh, 0, i))],
        out_shape=[jax.ShapeDtypeStruct((h, PEER_TOPK, t), jnp.int32),
                   jax.ShapeDtypeStruct((h, PEER_TOPK, t), F32)],
        compiler_params=_cparams(("parallel", "parallel")), name="peer_topk",
    )(q, sub_keys)
    return idx.reshape(h * PEER_TOPK, t), gate.reshape(h * PEER_TOPK, t)


N_PAIRS = PEER_HEADS * PEER_TOPK
TOK_BLOCK = LANES
PACKED_ROWS = 2 * SUBLANES


def _expert_row(tab_ref, e):
    start = pl.multiple_of((e >> 1) * PACKED_ROWS, PACKED_ROWS)
    both = tab_ref[pl.ds(start, PACKED_ROWS), :].astype(F32)
    return jnp.where((e & 1) == 1, both[SUBLANES:], both[:SUBLANES])


def _peer_act_body(idx_ref, h_ref, gate_ref, tab_ref, w_ref):
    lane = lax.broadcasted_iota(jnp.int32, (N_PAIRS, TOK_BLOCK), 1)

    def tok(t, tile):
        hv = h_ref[pl.ds(pl.multiple_of(t * SUBLANES, SUBLANES), SUBLANES), :]
        prods = [(_expert_row(tab_ref, idx_ref[p, t]) * hv)[None] for p in range(N_PAIRS)]
        pr = jnp.concatenate(prods, axis=0)
        col = jnp.sum(jnp.sum(pr, axis=1), axis=1, keepdims=True)
        return jnp.where(lane == t, col, tile)

    a = lax.fori_loop(0, TOK_BLOCK, tok, jnp.zeros((N_PAIRS, TOK_BLOCK), F32))
    w_ref[...] = jax.nn.gelu(a) * gate_ref[...]


def _peer_out_body(idx_ref, w_ref, tab_ref, x_ref, g_ref, o_ref):
    n_acc = 4

    def tok(t, carry):
        accs = [jnp.zeros((SUBLANES, LANES), F32) for _ in range(n_acc)]
        for p in range(N_PAIRS):
            accs[p % n_acc] = accs[p % n_acc] + w_ref[p, t] * _expert_row(tab_ref, idx_ref[p, t])
        y = (accs[0] + accs[1]) + (accs[2] + accs[3])
        rows = pl.ds(pl.multiple_of(t * SUBLANES, SUBLANES), SUBLANES)
        o_ref[rows, :] = x_ref[rows, :] + g_ref[0] * y
        return carry

    lax.fori_loop(0, TOK_BLOCK, tok, 0)


def _table_spec(n_rows):
    return pl.BlockSpec((n_rows, LANES), lambda i: (0, 0), pipeline_mode=pl.Buffered(1))


def peer_experts(x, h, idx, gate, u_tab, v_tab, g_f, *, seg_rows=8192):
    t, d = x.shape
    assert d == SUBLANES * LANES
    nblk = t // TOK_BLOCK
    rows_blk = TOK_BLOCK * SUBLANES
    smem_blk = pl.BlockSpec((N_PAIRS, TOK_BLOCK), lambda i: (0, i), memory_space=pltpu.SMEM)
    vm_blk = pl.BlockSpec((N_PAIRS, TOK_BLOCK), lambda i: (0, i))
    tok_rows = pl.BlockSpec((rows_blk, LANES), lambda i: (i, 0))
    w = pl.pallas_call(
        _peer_act_body, grid=(nblk,),
        in_specs=[smem_blk, tok_rows, vm_blk, _table_spec(u_tab.shape[0])],
        out_specs=vm_blk, out_shape=jax.ShapeDtypeStruct((N_PAIRS, t), F32),
        compiler_params=_cparams(("arbitrary",)), name="peer_activation",
    )(idx, h.reshape(t * SUBLANES, LANES), gate, u_tab)
    n_seg = g_f.shape[0]
    seg = functools.partial(_seg_of_block, tm=TOK_BLOCK, seg_rows=seg_rows, n_seg=n_seg)
    out = pl.pallas_call(
        _peer_out_body, grid=(nblk,),
        in_specs=[smem_blk, smem_blk, _table_spec(v_tab.shape[0]), tok_rows,
                  pl.BlockSpec((1, SUBLANES, LANES), lambda i: (seg(i), 0, 0))],
        out_specs=tok_rows, out_shape=jax.ShapeDtypeStruct((t * SUBLANES, LANES), F32),
        compiler_params=_cparams(("arbitrary",)), name="peer_output",
    )(idx, w, v_tab, x.reshape(t * SUBLANES, LANES), g_f.reshape(n_seg, SUBLANES, LANES))
    return out.reshape(t, d)


def _rms_body(x_ref, g_ref, o_ref):
    xf = x_ref[...]
    o_ref[...] = xf * lax.rsqrt(jnp.mean(xf * xf, axis=-1, keepdims=True) + EPS) * g_ref[...]


def final_rms_norm(x, g, *, tm=1024):
    m, d = x.shape
    return pl.pallas_call(
        _rms_body, grid=(m // tm,),
        in_specs=[pl.BlockSpec((tm, d), lambda i: (i, 0)), pl.BlockSpec((1, d), lambda i: (0, 0))],
        out_specs=pl.BlockSpec((tm, d), lambda i: (i, 0)),
        out_shape=jax.ShapeDtypeStruct((m, d), F32),
        compiler_params=_cparams(("parallel",)), name="final_rms_norm")(x, g)


def _rope_tables(seq, extra_rows):
    quarter = DIFF_HEAD_DIM // 4
    t = jnp.arange(seq)
    row = (t // GRID_W).astype(F32)
    col = (t % GRID_W).astype(F32)
    inv = ROPE_BASE ** (-jnp.arange(quarter, dtype=F32) / quarter)
    cos_r, sin_r = jnp.cos(row[:, None] * inv), jnp.sin(row[:, None] * inv)
    cos_c, sin_c = jnp.cos(col[:, None] * inv), jnp.sin(col[:, None] * inv)
    cos64 = jnp.concatenate([cos_r, cos_r, cos_c, cos_c], axis=1)
    sin64 = jnp.concatenate([-sin_r, sin_r, -sin_c, sin_c], axis=1)
    cos = jnp.concatenate([cos64, cos64], axis=1)
    sin = jnp.concatenate([sin64, sin64], axis=1)
    cos = jnp.concatenate([cos, jnp.ones((extra_rows, LANES), F32)], axis=0)
    sin = jnp.concatenate([sin, jnp.zeros((extra_rows, LANES), F32)], axis=0)
    return cos, sin


def _expert_table(t):
    return t.astype(BF16).reshape(t.shape[0] * SUBLANES, LANES)


def kernel(x, c, ctx, c_ctx, w_mod, b_mod, norm_mix_g, norm_ffn_g, even_w_in, even_w_out, na_rpb,
           odd_w_in, odd_w_out, diff_lambda_q1, diff_lambda_k1, diff_lambda_q2, diff_lambda_k2,
           diff_subln_g, peer_w_q, peer_sub_keys, peer_u, peer_v, final_norm_g):
    batch, seq, d = x.shape
    ctx_len = ctx.shape[1]
    depth = w_mod.shape[0]
    n_lat = batch * seq
    tm = 512
    xs = jnp.concatenate([x.reshape(n_lat, d), ctx.reshape(batch * ctx_len, d)], axis=0)
    cond = jnp.concatenate([c, c_ctx[None], jnp.zeros((SUBLANES - batch - 1, d), F32)], axis=0)
    cond = jax.nn.silu(cond)
    n_seg = batch + 1

    for layer in range(depth):
        last = layer == depth - 1
        jl = layer // 2
        mod = fused_matmul(cond, w_mod[layer].astype(BF16), tm=SUBLANES, tn=1024)[:n_seg] + b_mod[layer]
        sh_m, sc_m, g_m, sh_f, sc_f, g_f = [m_[:, None, :] for m_ in jnp.split(mod, 6, axis=-1)]
        norm_mix = (norm_mix_g[layer][None], sh_m, sc_m)
        norm_ffn = (norm_ffn_g[layer][None], sh_f, sc_f)
        if layer % 2 == 0:
            w_in = even_w_in[jl].astype(BF16)
            w_out = even_w_out[jl].astype(BF16)
            qkv = fused_matmul(xs, w_in[:, :3 * NA_WIDTH], norm=norm_mix, out_dtype=BF16, tm=tm)
            f = fused_matmul(xs, w_in[:, 3 * NA_WIDTH:], norm=norm_mix, tm=tm)
            attn = neighbourhood_attention(qkv, na_rpb[jl], batch=batch, seq=seq, ctx_len=ctx_len)
            four = fourier_long(f, batch=batch, seq=seq)
            if not last:
                attn = jnp.concatenate([attn, context_attention(qkv, batch=batch, seq=seq, ctx_len=ctx_len)], axis=0)
                four = jnp.concatenate([four, fourier_short(f, batch=batch, n=ctx_len, row_block0=n_lat // ctx_len)], axis=0)
            rows = attn.shape[0]
            xs_mix = fused_matmul(attn, w_out[:NA_WIDTH], res=(xs[:rows], g_m), tm=tm)
            xs_mix = fused_matmul(four, w_out[NA_WIDTH:], res=(xs_mix, g_m), tm=tm)
        else:
            lambda_init = 0.8 - 0.6 * math.exp(-0.3 * layer)
            lam = (jnp.exp(jnp.sum(diff_lambda_q1[jl] * diff_lambda_k1[jl]))
                   - jnp.exp(jnp.sum(diff_lambda_q2[jl] * diff_lambda_k2[jl])) + lambda_init).reshape(1)
            cos, sin = _rope_tables(seq, tm)
            qkv = fused_matmul(xs, odd_w_in[jl].astype(BF16), norm=norm_mix, out_dtype=BF16, tm=tm,
                               rope=(cos, sin, 2 * DIFF_HEADS * 2 * DIFF_HEAD_DIM, seq // tm, n_lat // tm))
            o = diff_attention(qkv, lam, diff_subln_g[jl][None], lambda_init, batch=batch, seq=seq, ctx_len=ctx_len)
            if not last:
                raise NotImplementedError("context output of a differential-attention layer")
            xs_mix = fused_matmul(o, odd_w_out[jl].astype(BF16), res=(xs[:n_lat], g_m), tm=tm)
        q, h = fused_matmul(xs_mix, peer_w_q[layer].astype(BF16), norm=norm_ffn, emit_h=True, tm=tm)
        idx, gate = peer_topk(q, peer_sub_keys[layer])
        xs = peer_experts(xs_mix, h, idx, gate, _expert_table(peer_u[layer]), _expert_table(peer_v[layer]), g_f)
        if not last and xs.shape[0] == n_lat:
            raise NotImplementedError("context stream dropped before the last layer")
    out = final_rms_norm(xs[:n_lat], final_norm_g[None])
    return out.reshape(batch, seq, d)
```

```python
import functools
import math

import jax
import jax.numpy as jnp
import numpy as np
from jax import lax
from jax.experimental import pallas as pl
from jax.experimental.pallas import tpu as pltpu

F32 = jnp.float32
BF16 = jnp.bfloat16

D_MODEL = 1024
GRID_W = 64
EPS = 1e-6
ROPE_BASE = 10000.0
NA_HEADS = 8
NA_HEAD_DIM = 64
NA_WIDTH = 512
NA_WIN_H = 8
NA_WIN_W = 16
FNET_GROUP_DIM = 128
DIFF_HEADS = 8
DIFF_HEAD_DIM = 64
PEER_HEADS = 8
PEER_N_KEYS = 128
PEER_TOPK = 16
LANES = 128
SUBLANES = 8
NEG = -1e30
VMEM_LIMIT = 56 * 1024 * 1024


def _cparams(sem, vmem=VMEM_LIMIT):
    return pltpu.CompilerParams(dimension_semantics=sem, vmem_limit_bytes=vmem)


def _mm_body(*refs, has_norm, emit_ht, rope_tiles, has_res, t_from):
    it = iter(refs)
    x_ref = next(it)
    if has_norm:
        g_ref, sh_ref, sc_ref = next(it), next(it), next(it)
    w_ref = next(it)
    if rope_tiles:
        cos_ref, sin_ref = next(it), next(it)
    if has_res:
        res_ref, gate_ref = next(it), next(it)
    o_ref = next(it)
    if emit_ht:
        ht_ref = next(it)
    if t_from is not None:
        ot_ref = next(it)
    if has_norm:
        xs_ref = next(it)
    j = pl.program_id(1)

    if has_norm:
        @pl.when(j == 0)
        def _():
            xf = x_ref[...]
            y = xf * lax.rsqrt(jnp.mean(xf * xf, axis=-1, keepdims=True) + EPS)
            y = y * g_ref[...] * (1.0 + sc_ref[0]) + sh_ref[0]
            xs_ref[...] = y.astype(BF16)
            if emit_ht:
                ht_ref[...] = y.T.astype(BF16)
        a = xs_ref[...]
    else:
        a = x_ref[...].astype(BF16)
    acc = jnp.dot(a, w_ref[...], preferred_element_type=F32)

    if t_from is not None:
        @pl.when(j >= t_from)
        def _():
            ot_ref[...] = acc.T.astype(ot_ref.dtype)

    def finish(v):
        if has_res:
            v = res_ref[...] + gate_ref[0] * v
        o_ref[...] = v.astype(o_ref.dtype)

    if rope_tiles:
        @pl.when(j < rope_tiles)
        def _():
            tn = acc.shape[1]
            reps = tn // LANES
            c = jnp.concatenate([cos_ref[...]] * reps, axis=1)
            s = jnp.concatenate([sin_ref[...]] * reps, axis=1)
            lane = lax.broadcasted_iota(jnp.int32, acc.shape, 1)
            first = (lane % 32) < 16
            partner = jnp.where(first, pltpu.roll(acc, tn - 16, 1), pltpu.roll(acc, 16, 1))
            finish(acc * c + partner * s)

        @pl.when(j >= rope_tiles)
        def _():
            finish(acc)
    else:
        finish(acc)


def _seg_of_block(i, tm, seg_rows, n_seg):
    return jnp.minimum((i * tm) // seg_rows, n_seg - 1)


def fused_matmul(x, w, *, norm=None, rope=None, res=None, emit_ht=False, t_cols=0, out_dtype=F32,
                 tm=512, tn=512, seg_rows=8192):
    m, k = x.shape
    n = w.shape[1]
    tm = min(tm, m)
    tn = min(tn, n)
    assert m % tm == 0 and n % tn == 0
    grid = (m // tm, n // tn)
    in_specs = [pl.BlockSpec((tm, k), lambda i, j: (i, 0))]
    args = [x]
    n_seg = 1
    if norm is not None:
        g, sh, sc = norm
        n_seg = sh.shape[0]
        seg = functools.partial(_seg_of_block, tm=tm, seg_rows=seg_rows, n_seg=n_seg)
        in_specs += [pl.BlockSpec((1, k), lambda i, j: (0, 0)),
                     pl.BlockSpec((1, 1, k), lambda i, j: (seg(i), 0, 0)),
                     pl.BlockSpec((1, 1, k), lambda i, j: (seg(i), 0, 0))]
        args += [g, sh, sc]
    in_specs.append(pl.BlockSpec((k, tn), lambda i, j: (0, j)))
    args.append(w)
    rope_tiles = 0
    if rope is not None:
        cos, sin, n_rope_cols, n_pos_blocks, n_lat_blocks = rope
        rope_tiles = n_rope_cols // tn
        pos = lambda i, j: (jnp.where(i < n_lat_blocks, i % n_pos_blocks, n_pos_blocks), 0)
        in_specs += [pl.BlockSpec((tm, LANES), pos), pl.BlockSpec((tm, LANES), pos)]
        args += [cos, sin]
    if res is not None:
        r, gate = res
        n_seg_r = gate.shape[0]
        segr = functools.partial(_seg_of_block, tm=tm, seg_rows=seg_rows, n_seg=n_seg_r)
        in_specs += [pl.BlockSpec((tm, tn), lambda i, j: (i, j)),
                     pl.BlockSpec((1, 1, tn), lambda i, j: (segr(i), 0, j))]
        args += [r, gate]
    out_shape = [jax.ShapeDtypeStruct((m, n), out_dtype)]
    out_specs = [pl.BlockSpec((tm, tn), lambda i, j: (i, j))]
    if emit_ht:
        out_shape.append(jax.ShapeDtypeStruct((k, m), BF16))
        out_specs.append(pl.BlockSpec((k, tm), lambda i, j: (0, i)))
    t_from = None
    if t_cols:
        assert t_cols % tn == 0
        t_from = (n - t_cols) // tn
        out_shape.append(jax.ShapeDtypeStruct((t_cols, m), BF16))
        out_specs.append(pl.BlockSpec((tn, tm), lambda i, j: (jnp.maximum(j - t_from, 0), i)))
    scratch = [pltpu.VMEM((tm, k), BF16)] if norm is not None else []
    body = functools.partial(_mm_body, has_norm=norm is not None, emit_ht=emit_ht,
                             rope_tiles=rope_tiles, has_res=res is not None, t_from=t_from)
    outs = pl.pallas_call(
        body, grid=grid, in_specs=in_specs, out_specs=out_specs, out_shape=out_shape,
        scratch_shapes=scratch, compiler_params=_cparams(("parallel", "arbitrary")),
        name="fused_matmul")(*args)
    return outs if len(outs) > 1 else outs[0]


def _na_body(q_ref, k_ref, v_ref, kc_ref, vc_ref, bias_ref, o_ref, *, rows):
    lane = lax.broadcasted_iota(jnp.int32, (GRID_W, LANES), 1)
    head_masks = [lane < NA_HEAD_DIM, lane >= NA_HEAD_DIM]
    kc = kc_ref[...]
    vc = vc_ref[...]
    scale = NA_HEAD_DIM ** -0.5
    nt = (((1,), (1,)), ((), ()))

    def row_body(r, carry):
        r_start = jnp.clip(r - NA_WIN_H // 2, 0, rows - NA_WIN_H)
        off = r_start - r + NA_WIN_H - 1
        q = q_ref[pl.ds(pl.multiple_of(r * GRID_W, GRID_W), GRID_W), :]
        k_start = pl.multiple_of(r_start * GRID_W, GRID_W)
        kw = k_ref[pl.ds(k_start, NA_WIN_H * GRID_W), :]
        vw = v_ref[pl.ds(k_start, NA_WIN_H * GRID_W), :]
        outs = []
        for hh in range(2):
            qm = jnp.where(head_masks[hh], q, jnp.zeros_like(q))
            s_nb = lax.dot_general(qm, kw, nt, preferred_element_type=F32) * scale + bias_ref[off, hh]
            s_cx = lax.dot_general(qm, kc, nt, preferred_element_type=F32) * scale
            m = jnp.maximum(jnp.max(s_nb, axis=1, keepdims=True), jnp.max(s_cx, axis=1, keepdims=True))
            p_nb = jnp.exp(s_nb - m)
            p_cx = jnp.exp(s_cx - m)
            l = jnp.sum(p_nb, axis=1, keepdims=True) + jnp.sum(p_cx, axis=1, keepdims=True)
            o = (jnp.dot(p_nb.astype(BF16), vw, preferred_element_type=F32)
                 + jnp.dot(p_cx.astype(BF16), vc, preferred_element_type=F32))
            outs.append(o / l)
        o_ref[pl.ds(pl.multiple_of(r * GRID_W, GRID_W), GRID_W), :] = jnp.where(
            head_masks[0], outs[0], outs[1]).astype(o_ref.dtype)
        return carry

    lax.fori_loop(0, rows, row_body, 0)


def _na_bias_table(rpb):
    col = np.arange(GRID_W)
    col_start = np.clip(col - NA_WIN_W // 2, 0, GRID_W - NA_WIN_W)
    kc = np.arange(GRID_W)
    inside = (kc[None, :] >= col_start[:, None]) & (kc[None, :] < col_start[:, None] + NA_WIN_W)
    dc = np.clip(kc[None, :] - col[:, None] + NA_WIN_W - 1, 0, 2 * NA_WIN_W - 2)
    per_dr = jnp.where(inside[None, None], rpb[:, :, dc], NEG)
    tabs = []
    for off in range(NA_WIN_H):
        blk = per_dr[:, off:off + NA_WIN_H]
        tabs.append(blk.transpose(0, 2, 1, 3).reshape(NA_HEADS, GRID_W, NA_WIN_H * GRID_W))
    return jnp.stack(tabs)


def neighbourhood_attention(qkv, rpb, *, batch, seq, ctx_len):
    rows = seq // GRID_W
    bias = _na_bias_table(rpb)
    hp = NA_HEADS // 2
    ctx_blk0 = (batch * seq) // ctx_len
    body = functools.partial(_na_body, rows=rows)
    return pl.pallas_call(
        body, grid=(batch, hp),
        in_specs=[pl.BlockSpec((seq, LANES), lambda b, h: (b, h)),
                  pl.BlockSpec((seq, LANES), lambda b, h: (b, hp + h)),
                  pl.BlockSpec((seq, LANES), lambda b, h: (b, 2 * hp + h)),
                  pl.BlockSpec((ctx_len, LANES), lambda b, h: (ctx_blk0 + b, hp + h)),
                  pl.BlockSpec((ctx_len, LANES), lambda b, h: (ctx_blk0 + b, 2 * hp + h)),
                  pl.BlockSpec((NA_WIN_H, 2, GRID_W, NA_WIN_H * GRID_W), lambda b, h: (0, h, 0, 0))],
        out_specs=pl.BlockSpec((seq, LANES), lambda b, h: (b, h)),
        out_shape=jax.ShapeDtypeStruct((batch * seq, NA_WIDTH), BF16),
        compiler_params=_cparams(("parallel", "parallel")), name="neighbourhood_attention",
    )(qkv, qkv, qkv, qkv, qkv, bias)


def _ctx_attn_body(q_ref, k_ref, v_ref, o_ref):
    lane = lax.broadcasted_iota(jnp.int32, q_ref.shape, 1)
    q = q_ref[...]
    k = k_ref[...]
    v = v_ref[...]
    scale = NA_HEAD_DIM ** -0.5
    outs = []
    for hh in range(2):
        msk = (lane < NA_HEAD_DIM) if hh == 0 else (lane >= NA_HEAD_DIM)
        qm = jnp.where(msk, q, jnp.zeros_like(q))
        s = lax.dot_general(qm, k, (((1,), (1,)), ((), ())), preferred_element_type=F32) * scale
        m = jnp.max(s, axis=1, keepdims=True)
        p = jnp.exp(s - m)
        l = jnp.sum(p, axis=1, keepdims=True)
        outs.append(jnp.dot(p.astype(BF16), v, preferred_element_type=F32) / l)
    o_ref[...] = jnp.where(lane < NA_HEAD_DIM, outs[0], outs[1]).astype(o_ref.dtype)


def context_attention(qkv, *, batch, seq, ctx_len):
    hp = NA_HEADS // 2
    ctx_blk0 = (batch * seq) // ctx_len
    return pl.pallas_call(
        _ctx_attn_body, grid=(batch, hp),
        in_specs=[pl.BlockSpec((ctx_len, LANES), lambda b, h: (ctx_blk0 + b, h)),
                  pl.BlockSpec((ctx_len, LANES), lambda b, h: (ctx_blk0 + b, hp + h)),
                  pl.BlockSpec((ctx_len, LANES), lambda b, h: (ctx_blk0 + b, 2 * hp + h))],
        out_specs=pl.BlockSpec((ctx_len, LANES), lambda b, h: (b, h)),
        out_shape=jax.ShapeDtypeStruct((batch * ctx_len, NA_WIDTH), BF16),
        compiler_params=_cparams(("parallel", "parallel")), name="context_attention",
    )(qkv, qkv, qkv)


def _dft_cs(n):
    k = np.arange(n)
    ang = 2.0 * np.pi * ((k[:, None] * k[None, :]) % n) / n
    return np.cos(ang), np.sin(ang)


def _fourier_long_body(f_ref, wch_ref, ma_ref, mb_ref, o_ref, pr_ref, pi_ref, z_ref, *, n1, n2):
    c = FNET_GROUP_DIM
    p = jnp.dot(f_ref[...].astype(BF16), wch_ref[...], preferred_element_type=F32)
    pr_ref[...] = p[:, :c]
    pi_ref[...] = p[:, c:]

    def stage_a(j, carry):
        rows = pl.ds(j, n1, stride=n2)
        x = jnp.concatenate([pr_ref[rows, :], pi_ref[rows, :]], axis=1).astype(BF16)
        y = jnp.dot(ma_ref[j], x, preferred_element_type=F32)
        zr = y[:n1, :c] - y[n1:, c:]
        zi = y[n1:, :c] + y[:n1, c:]
        z_ref[pl.ds(j, n1, stride=2 * n2), :] = zr
        z_ref[pl.ds(n2 + j, n1, stride=2 * n2), :] = zi
        return carry

    lax.fori_loop(0, n2, stage_a, 0)

    def stage_b(k1, carry):
        z = z_ref[pl.ds(pl.multiple_of(k1 * 2 * n2, 2 * n2), 2 * n2), :].astype(BF16)
        y = jnp.dot(mb_ref[...], z, preferred_element_type=F32)
        o_ref[pl.ds(k1, n2, stride=n1), :] = y.astype(o_ref.dtype)
        return carry

    lax.fori_loop(0, n1, stage_b, 0)


def fourier_long(f, *, batch, seq):
    c = FNET_GROUP_DIM
    groups = f.shape[1] // c
    n2 = 128
    n1 = seq // n2
    norm = 1.0 / math.sqrt(seq * c)
    cc, sc = _dft_cs(c)
    wch = np.concatenate([cc, -sc], axis=1)
    c1, s1 = _dft_cs(n1)
    j = np.arange(n2)
    k1 = np.arange(n1)
    tw = 2.0 * np.pi * (j[:, None] * k1[None, :]) / seq
    ar = np.cos(tw)[:, :, None] * c1[None] - np.sin(tw)[:, :, None] * s1[None]
    ai = -(np.cos(tw)[:, :, None] * s1[None] + np.sin(tw)[:, :, None] * c1[None])
    ma = np.concatenate([ar, ai], axis=1)
    c2, s2 = _dft_cs(n2)
    mb = np.concatenate([c2, s2], axis=1) * norm
    body = functools.partial(_fourier_long_body, n1=n1, n2=n2)
    return pl.pallas_call(
        body, grid=(batch, groups),
        in_specs=[pl.BlockSpec((seq, c), lambda b, g: (b, g)),
                  pl.BlockSpec((c, 2 * c), lambda b, g: (0, 0)),
                  pl.BlockSpec((n2, 2 * n1, n1), lambda b, g: (0, 0, 0)),
                  pl.BlockSpec((n2, 2 * n2), lambda b, g: (0, 0))],
        out_specs=pl.BlockSpec((seq, c), lambda b, g: (b, g)),
        out_shape=jax.ShapeDtypeStruct((batch * seq, groups * c), F32),
        scratch_shapes=[pltpu.VMEM((seq, c), F32), pltpu.VMEM((seq, c), F32), pltpu.VMEM((2 * seq, c), F32)],
        compiler_params=_cparams(("parallel", "parallel")), name="fourier_long",
    )(f, jnp.asarray(wch, BF16), jnp.asarray(ma, BF16), jnp.asarray(mb, BF16))


def _fourier_short_body(f_ref, wch_ref, mp_ref, o_ref):
    c = FNET_GROUP_DIM
    p = jnp.dot(f_ref[...].astype(BF16), wch_ref[...], preferred_element_type=F32)
    z = jnp.concatenate([p[:, :c], p[:, c:]], axis=0).astype(BF16)
    o_ref[...] = jnp.dot(mp_ref[...], z, preferred_element_type=F32).astype(o_ref.dtype)


def fourier_short(f, *, batch, n, row_block0):
    c = FNET_GROUP_DIM
    groups = f.shape[1] // c
    norm = 1.0 / math.sqrt(n * c)
    cc, sc = _dft_cs(c)
    wch = np.concatenate([cc, -sc], axis=1)
    cn, sn = _dft_cs(n)
    mp = np.concatenate([cn, sn], axis=1) * norm
    return pl.pallas_call(
        _fourier_short_body, grid=(batch, groups),
        in_specs=[pl.BlockSpec((n, c), lambda b, g: (row_block0 + b, g)),
                  pl.BlockSpec((c, 2 * c), lambda b, g: (0, 0)),
                  pl.BlockSpec((n, 2 * n), lambda b, g: (0, 0))],
        out_specs=pl.BlockSpec((n, c), lambda b, g: (b, g)),
        out_shape=jax.ShapeDtypeStruct((batch * n, groups * c), F32),
        compiler_params=_cparams(("parallel", "parallel")), name="fourier_short",
    )(f, jnp.asarray(wch, BF16), jnp.asarray(mp, BF16))


LOG2E = 1.4426950408889634


def _diff_body(lam_ref, q_ref, kl_ref, vtl_ref, kc_ref, vtc_ref, g_ref, o_ref,
               qs_ref, m_ref, l_ref, acc_ref, *, nk, out_scale):
    j = pl.program_id(3)
    d = DIFF_HEAD_DIM

    @pl.when(j == 0)
    def _():
        q = q_ref[...].astype(F32) * (d ** -0.5 * LOG2E)
        lane = lax.broadcasted_iota(jnp.int32, q.shape, 1)
        qs_ref[0] = jnp.where(lane < d, q, 0.0).astype(BF16)
        qs_ref[1] = jnp.where(lane >= d, q, 0.0).astype(BF16)
        m_ref[...] = jnp.full(m_ref.shape, NEG, F32)
        l_ref[...] = jnp.zeros(l_ref.shape, F32)
        acc_ref[...] = jnp.zeros(acc_ref.shape, F32)

    def step(k, vt):
        for comp in range(2):
            st = lax.dot_general(k, qs_ref[comp], (((1,), (1,)), ((), ())),
                                 preferred_element_type=F32)
            m_old = m_ref[comp]
            m_new = jnp.maximum(m_old, jnp.max(st, axis=0, keepdims=True))
            alpha = jnp.exp2(m_old - m_new)
            p = jnp.exp2(st - m_new)
            l_ref[comp] = alpha * l_ref[comp] + jnp.sum(p, axis=0, keepdims=True)
            acc_ref[comp] = alpha * acc_ref[comp] + jnp.dot(vt, p.astype(BF16), preferred_element_type=F32)
            m_ref[comp] = m_new

    @pl.when(j < nk)
    def _():
        step(kl_ref[...], vtl_ref[...])

    @pl.when(j == nk)
    def _():
        step(kc_ref[...], vtc_ref[...])
        lam = lam_ref[0]
        ot = acc_ref[0] / l_ref[0] - lam * (acc_ref[1] / l_ref[1])
        yt = ot * lax.rsqrt(jnp.mean(ot * ot, axis=0, keepdims=True) + EPS)
        o_ref[...] = (yt.T * g_ref[...] * out_scale).astype(o_ref.dtype)


def diff_attention(qkv, vt, lam, subln_g, lambda_init, *, batch, seq, ctx_len, tq=512, tk=512):
    h = DIFF_HEADS
    nq = seq // tq
    nk = seq // tk
    ctx_blk0 = (batch * seq) // ctx_len
    body = functools.partial(_diff_body, nk=nk, out_scale=1.0 - lambda_init)
    lat = lambda b, hh, i, j: b * nk + jnp.minimum(j, nk - 1)
    return pl.pallas_call(
        body, grid=(batch, h, nq, nk + 1),
        in_specs=[pl.BlockSpec(memory_space=pltpu.SMEM),
                  pl.BlockSpec((tq, LANES), lambda b, hh, i, j: (b * nq + i, hh)),
                  pl.BlockSpec((tk, LANES), lambda b, hh, i, j: (lat(b, hh, i, j), h + hh)),
                  pl.BlockSpec((LANES, tk), lambda b, hh, i, j: (hh, lat(b, hh, i, j))),
                  pl.BlockSpec((ctx_len, LANES), lambda b, hh, i, j: (ctx_blk0 + b, h + hh)),
                  pl.BlockSpec((LANES, ctx_len), lambda b, hh, i, j: (hh, ctx_blk0 + b)),
                  pl.BlockSpec((1, LANES), lambda b, hh, i, j: (0, 0))],
        out_specs=pl.BlockSpec((tq, LANES), lambda b, hh, i, j: (b * nq + i, hh)),
        out_shape=jax.ShapeDtypeStruct((batch * seq, h * LANES), BF16),
        scratch_shapes=[pltpu.VMEM((2, tq, LANES), BF16), pltpu.VMEM((2, 1, tq), F32),
                        pltpu.VMEM((2, 1, tq), F32), pltpu.VMEM((2, LANES, tq), F32)],
        compiler_params=_cparams(("parallel", "parallel", "parallel", "arbitrary")),
        name="diff_attention",
    )(lam, qkv, qkv, vt, qkv, vt, subln_g)


def _extract16(s, pos):
    big = jnp.int32(2 ** 30)
    rank = jnp.full(s.shape, float(PEER_TOPK), F32)
    vals = []
    for r in range(PEER_TOPK):
        m = jnp.max(s, axis=0, keepdims=True)
        sel = jnp.min(jnp.where(s == m, pos, big), axis=0, keepdims=True)
        hit = pos == sel
        vals.append(m)
        rank = jnp.where(hit, float(r), rank)
        s = jnp.where(hit, -jnp.inf, s)
    return vals, rank, rank < float(PEER_TOPK)


_CAND_GROUPS = ((0, 16), (1, 8), (2, 5), (3, 4), (4, 3), (5, 2), (6, 2), (7, 2))


def _peer_topk_body(q_ref, keys_ref, r1_ref, e1_ref, n0_ref, c0_ref):
    q = q_ref[...]
    kd = q.shape[1] // 2
    tb = q.shape[0]
    res = []
    for half in range(2):
        s = lax.dot_general(keys_ref[0, half], q[:, half * kd:(half + 1) * kd],
                            (((1,), (1,)), ((), ())), preferred_element_type=F32)
        pos = lax.broadcasted_iota(jnp.int32, s.shape, 0)
        vals, rank, _ = _extract16(s, pos)
        res.append((s, vals, rank))
    (s0, v0, rank0), (s1, v1, rank1) = res
    v1_lo = jnp.concatenate(v1[:SUBLANES], axis=0)
    v1_hi = jnp.concatenate(v1[SUBLANES:], axis=0)
    row8 = lax.broadcasted_iota(jnp.int32, (SUBLANES, tb), 0)
    pieces, poses = [], []
    for a, nb in _CAND_GROUPS:
        if nb > SUBLANES:
            pieces += [v0[a] + v1_lo, v0[a] + v1_hi]
            poses += [a * PEER_TOPK + row8, a * PEER_TOPK + SUBLANES + row8]
        else:
            pieces.append(jnp.where(row8 < nb, v0[a] + v1_lo, -jnp.inf))
            poses.append(a * PEER_TOPK + row8)
    pieces.append(jnp.concatenate(v0[SUBLANES:], axis=0) + v1[0])
    poses.append((row8 + SUBLANES) * PEER_TOPK)
    cand = jnp.concatenate(pieces, axis=0)
    cpos = jnp.concatenate(poses, axis=0)
    best, _, taken = _extract16(cand, cpos)
    z = sum(jnp.exp(b - best[0]) for b in best)
    takenf = taken.astype(F32)
    counts = []
    row = 0
    for a, nb in _CAND_GROUPS:
        rows = 2 * SUBLANES if nb > SUBLANES else SUBLANES
        counts.append(jnp.sum(takenf[row:row + rows], axis=0, keepdims=True))
        row += rows
    counts += [takenf[row + i:row + i + 1] for i in range(SUBLANES)]
    n0 = jnp.zeros(s0.shape, F32)
    for a in range(PEER_TOPK):
        n0 = jnp.where(rank0 == float(a), counts[a], n0)
    r1_ref[0] = rank1
    e1_ref[0] = jnp.exp(s1 - v1[0])
    n0_ref[0] = n0
    c0_ref[0] = jnp.exp(s0 - v0[0]) / z


def peer_topk(q, sub_keys, *, tb=256):
    t = q.shape[0]
    h = PEER_HEADS
    spec = pl.BlockSpec((1, PEER_N_KEYS, tb), lambda i, hh: (hh, 0, i))
    shape = jax.ShapeDtypeStruct((h, PEER_N_KEYS, t), F32)
    return pl.pallas_call(
        _peer_topk_body, grid=(t // tb, h),
        in_specs=[pl.BlockSpec((tb, 2 * LANES), lambda i, hh: (i, hh)),
                  pl.BlockSpec((1, 2, PEER_N_KEYS, LANES), lambda i, hh: (hh, 0, 0, 0))],
        out_specs=[spec] * 4, out_shape=[shape] * 4,
        compiler_params=_cparams(("parallel", "parallel")), name="peer_topk",
    )(q, sub_keys)


def _peer_mlp_body(ht_ref, u_ref, vt_ref, r1_ref, e1_ref, n0_ref, c0_ref, x_ref, gate_ref, o_ref, acc_ref,
                   *, blocks_per_step):
    c = pl.program_id(1)

    @pl.when(c == 0)
    def _():
        acc_ref[...] = jnp.zeros(acc_ref.shape, F32)

    at = jnp.dot(u_ref[...], ht_ref[...], preferred_element_type=F32)
    tiles = []
    for ib in range(blocks_per_step):
        i = c * blocks_per_step + ib
        g = None
        for h in range(PEER_HEADS):
            n0 = n0_ref[h, pl.ds(i, 1), :]
            c0 = c0_ref[h, pl.ds(i, 1), :]
            term = jnp.where(r1_ref[h] < n0, e1_ref[h], 0.0) * c0
            g = term if g is None else g + term
        tiles.append(g)
    gt = jnp.concatenate(tiles, axis=0)
    wt = (jax.nn.gelu(at) * gt).astype(BF16)
    acc_ref[...] += jnp.dot(vt_ref[...], wt, preferred_element_type=F32)

    @pl.when(c == pl.num_programs(1) - 1)
    def _():
        o_ref[...] = x_ref[...] + gate_ref[0] * acc_ref[...].T


def peer_experts(x, ht, routing, u, vt, g_f, *, tm=512, te=1024, seg_rows=8192):
    t, d = x.shape
    n_exp = u.shape[0]
    n_seg = g_f.shape[0]
    seg = functools.partial(_seg_of_block, tm=tm, seg_rows=seg_rows, n_seg=n_seg)
    route_spec = pl.BlockSpec((PEER_HEADS, PEER_N_KEYS, tm), lambda i, c: (0, 0, i))
    body = functools.partial(_peer_mlp_body, blocks_per_step=te // PEER_N_KEYS)
    return pl.pallas_call(
        body, grid=(t // tm, n_exp // te),
        in_specs=[pl.BlockSpec((d, tm), lambda i, c: (0, i)),
                  pl.BlockSpec((te, d), lambda i, c: (c, 0)),
                  pl.BlockSpec((d, te), lambda i, c: (0, c)),
                  route_spec, route_spec, route_spec, route_spec,
                  pl.BlockSpec((tm, d), lambda i, c: (i, 0)),
                  pl.BlockSpec((1, 1, d), lambda i, c: (seg(i), 0, 0))],
        out_specs=pl.BlockSpec((tm, d), lambda i, c: (i, 0)),
        out_shape=jax.ShapeDtypeStruct((t, d), F32),
        scratch_shapes=[pltpu.VMEM((d, tm), F32)],
        compiler_params=_cparams(("parallel", "arbitrary")), name="peer_experts",
    )(ht, u, vt, *routing, x, g_f)


def _rms_body(x_ref, g_ref, o_ref):
    xf = x_ref[...]
    o_ref[...] = xf * lax.rsqrt(jnp.mean(xf * xf, axis=-1, keepdims=True) + EPS) * g_ref[...]


def final_rms_norm(x, g, *, tm=1024):
    m, d = x.shape
    return pl.pallas_call(
        _rms_body, grid=(m // tm,),
        in_specs=[pl.BlockSpec((tm, d), lambda i: (i, 0)), pl.BlockSpec((1, d), lambda i: (0, 0))],
        out_specs=pl.BlockSpec((tm, d), lambda i: (i, 0)),
        out_shape=jax.ShapeDtypeStruct((m, d), F32),
        compiler_params=_cparams(("parallel",)), name="final_rms_norm")(x, g)


def _rope_tables(seq, extra_rows):
    quarter = DIFF_HEAD_DIM // 4
    t = jnp.arange(seq)
    row = (t // GRID_W).astype(F32)
    col = (t % GRID_W).astype(F32)
    inv = ROPE_BASE ** (-jnp.arange(quarter, dtype=F32) / quarter)
    cos_r, sin_r = jnp.cos(row[:, None] * inv), jnp.sin(row[:, None] * inv)
    cos_c, sin_c = jnp.cos(col[:, None] * inv), jnp.sin(col[:, None] * inv)
    cos64 = jnp.concatenate([cos_r, cos_r, cos_c, cos_c], axis=1)
    sin64 = jnp.concatenate([-sin_r, sin_r, -sin_c, sin_c], axis=1)
    cos = jnp.concatenate([cos64, cos64], axis=1)
    sin = jnp.concatenate([sin64, sin64], axis=1)
    cos = jnp.concatenate([cos, jnp.ones((extra_rows, LANES), F32)], axis=0)
    sin = jnp.concatenate([sin, jnp.zeros((extra_rows, LANES), F32)], axis=0)
    return cos, sin


def kernel(x, c, ctx, c_ctx, w_mod, b_mod, norm_mix_g, norm_ffn_g, even_w_in, even_w_out, na_rpb,
           odd_w_in, odd_w_out, diff_lambda_q1, diff_lambda_k1, diff_lambda_q2, diff_lambda_k2,
           diff_subln_g, peer_w_q, peer_sub_keys, peer_u, peer_v, final_norm_g):
    batch, seq, d = x.shape
    ctx_len = ctx.shape[1]
    depth = w_mod.shape[0]
    n_lat = batch * seq
    tm = 512
    xs = jnp.concatenate([x.reshape(n_lat, d), ctx.reshape(batch * ctx_len, d)], axis=0)
    cond = jnp.concatenate([c, c_ctx[None], jnp.zeros((SUBLANES - batch - 1, d), F32)], axis=0)
    cond = jax.nn.silu(cond)
    n_seg = batch + 1

    for layer in range(depth):
        last = layer == depth - 1
        jl = layer // 2
        mod = fused_matmul(cond, w_mod[layer].astype(BF16), tm=SUBLANES, tn=1024)[:n_seg] + b_mod[layer]
        sh_m, sc_m, g_m, sh_f, sc_f, g_f = [m_[:, None, :] for m_ in jnp.split(mod, 6, axis=-1)]
        norm_mix = (norm_mix_g[layer][None], sh_m, sc_m)
        norm_ffn = (norm_ffn_g[layer][None], sh_f, sc_f)
        if layer % 2 == 0:
            w_in = even_w_in[jl].astype(BF16)
            w_out = even_w_out[jl].astype(BF16)
            qkv = fused_matmul(xs, w_in[:, :3 * NA_WIDTH], norm=norm_mix, out_dtype=BF16, tm=tm)
            f = fused_matmul(xs, w_in[:, 3 * NA_WIDTH:], norm=norm_mix, tm=tm)
            attn = neighbourhood_attention(qkv, na_rpb[jl], batch=batch, seq=seq, ctx_len=ctx_len)
            four = fourier_long(f, batch=batch, seq=seq)
            if not last:
                attn = jnp.concatenate([attn, context_attention(qkv, batch=batch, seq=seq, ctx_len=ctx_len)], axis=0)
                four = jnp.concatenate([four, fourier_short(f, batch=batch, n=ctx_len, row_block0=n_lat // ctx_len)], axis=0)
            rows = attn.shape[0]
            xs_mix = fused_matmul(attn, w_out[:NA_WIDTH], res=(xs[:rows], g_m), tm=tm)
            xs_mix = fused_matmul(four, w_out[NA_WIDTH:], res=(xs_mix, g_m), tm=tm)
        else:
            lambda_init = 0.8 - 0.6 * math.exp(-0.3 * layer)
            lam = (jnp.exp(jnp.sum(diff_lambda_q1[jl] * diff_lambda_k1[jl]))
                   - jnp.exp(jnp.sum(diff_lambda_q2[jl] * diff_lambda_k2[jl])) + lambda_init).reshape(1)
            cos, sin = _rope_tables(seq, tm)
            v_width = DIFF_HEADS * 2 * DIFF_HEAD_DIM
            qkv, vt = fused_matmul(xs, odd_w_in[jl].astype(BF16), norm=norm_mix, out_dtype=BF16, tm=tm,
                                   rope=(cos, sin, 2 * DIFF_HEADS * 2 * DIFF_HEAD_DIM, seq // tm, n_lat // tm),
                                   t_cols=v_width)
            o = diff_attention(qkv, vt, lam, diff_subln_g[jl][None], lambda_init, batch=batch, seq=seq, ctx_len=ctx_len)
            if not last:
                raise NotImplementedError("context output of a differential-attention layer")
            xs_mix = fused_matmul(o, odd_w_out[jl].astype(BF16), res=(xs[:n_lat], g_m), tm=tm)
        q, ht = fused_matmul(xs_mix, peer_w_q[layer].astype(BF16), norm=norm_ffn, emit_ht=True, tm=tm)
        routing = peer_topk(q, peer_sub_keys[layer])
        xs = peer_experts(xs_mix, ht, routing, peer_u[layer].astype(BF16), peer_v[layer].astype(BF16).T, g_f)
        if not last and xs.shape[0] == n_lat:
            raise NotImplementedError("context stream dropped before the last layer")
    out = final_rms_norm(xs[:n_lat], final_norm_g[None])
    return out.reshape(batch, seq, d)
```

```python
import functools
import math

import jax
import jax.numpy as jnp
import numpy as np
from jax import lax
from jax.experimental import pallas as pl
from jax.experimental.pallas import tpu as pltpu

F32 = jnp.float32
BF16 = jnp.bfloat16

D_MODEL = 1024
GRID_W = 64
EPS = 1e-6
ROPE_BASE = 10000.0
NA_HEADS = 8
NA_HEAD_DIM = 64
NA_WIDTH = 512
NA_WIN_H = 8
NA_WIN_W = 16
FNET_GROUP_DIM = 128
DIFF_HEADS = 8
DIFF_HEAD_DIM = 64
PEER_HEADS = 8
PEER_N_KEYS = 128
PEER_TOPK = 16
LANES = 128
SUBLANES = 8
NEG = -1e30
VMEM_LIMIT = 56 * 1024 * 1024


def _cparams(sem, vmem=VMEM_LIMIT):
    return pltpu.CompilerParams(dimension_semantics=sem, vmem_limit_bytes=vmem)


def _mm_body(*refs, has_norm, emit_ht, rope_tiles, has_res, t_from):
    it = iter(refs)
    x_ref = next(it)
    if has_norm:
        g_ref, sh_ref, sc_ref = next(it), next(it), next(it)
    w_ref = next(it)
    if rope_tiles:
        cos_ref, sin_ref = next(it), next(it)
    if has_res:
        res_ref, gate_ref = next(it), next(it)
    o_ref = next(it)
    if emit_ht:
        ht_ref = next(it)
    if t_from is not None:
        ot_ref = next(it)
    if has_norm:
        xs_ref = next(it)
    j = pl.program_id(1)

    if has_norm:
        @pl.when(j == 0)
        def _():
            xf = x_ref[...]
            y = xf * lax.rsqrt(jnp.mean(xf * xf, axis=-1, keepdims=True) + EPS)
            y = y * g_ref[...] * (1.0 + sc_ref[0]) + sh_ref[0]
            xs_ref[...] = y.astype(BF16)
            if emit_ht:
                ht_ref[...] = y.T.astype(BF16)
        a = xs_ref[...]
    else:
        a = x_ref[...].astype(BF16)
    acc = jnp.dot(a, w_ref[...], preferred_element_type=F32)

    if t_from is not None:
        @pl.when(j >= t_from)
        def _():
            ot_ref[...] = acc.T.astype(ot_ref.dtype)

    def finish(v):
        if has_res:
            v = res_ref[...] + gate_ref[0] * v
        o_ref[...] = v.astype(o_ref.dtype)

    if rope_tiles:
        @pl.when(j < rope_tiles)
        def _():
            tn = acc.shape[1]
            reps = tn // LANES
            c = jnp.concatenate([cos_ref[...]] * reps, axis=1)
            s = jnp.concatenate([sin_ref[...]] * reps, axis=1)
            lane = lax.broadcasted_iota(jnp.int32, acc.shape, 1)
            first = (lane % 32) < 16
            partner = jnp.where(first, pltpu.roll(acc, tn - 16, 1), pltpu.roll(acc, 16, 1))
            finish(acc * c + partner * s)

        @pl.when(j >= rope_tiles)
        def _():
            finish(acc)
    else:
        finish(acc)


def _seg_of_block(i, tm, seg_rows, n_seg):
    return jnp.minimum((i * tm) // seg_rows, n_seg - 1)


def fused_matmul(x, w, *, norm=None, rope=None, res=None, emit_ht=False, t_cols=0, out_dtype=F32,
                 tm=512, tn=512, seg_rows=8192):
    m, k = x.shape
    n = w.shape[1]
    tm = min(tm, m)
    tn = min(tn, n)
    assert m % tm == 0 and n % tn == 0
    grid = (m // tm, n // tn)
    in_specs = [pl.BlockSpec((tm, k), lambda i, j: (i, 0))]
    args = [x]
    n_seg = 1
    if norm is not None:
        g, sh, sc = norm
        n_seg = sh.shape[0]
        seg = functools.partial(_seg_of_block, tm=tm, seg_rows=seg_rows, n_seg=n_seg)
        in_specs += [pl.BlockSpec((1, k), lambda i, j: (0, 0)),
                     pl.BlockSpec((1, 1, k), lambda i, j: (seg(i), 0, 0)),
                     pl.BlockSpec((1, 1, k), lambda i, j: (seg(i), 0, 0))]
        args += [g, sh, sc]
    in_specs.append(pl.BlockSpec((k, tn), lambda i, j: (0, j)))
    args.append(w)
    rope_tiles = 0
    if rope is not None:
        cos, sin, n_rope_cols, n_pos_blocks, n_lat_blocks = rope
        rope_tiles = n_rope_cols // tn
        pos = lambda i, j: (jnp.where(i < n_lat_blocks, i % n_pos_blocks, n_pos_blocks), 0)
        in_specs += [pl.BlockSpec((tm, LANES), pos), pl.BlockSpec((tm, LANES), pos)]
        args += [cos, sin]
    if res is not None:
        r, gate = res
        n_seg_r = gate.shape[0]
        segr = functools.partial(_seg_of_block, tm=tm, seg_rows=seg_rows, n_seg=n_seg_r)
        in_specs += [pl.BlockSpec((tm, tn), lambda i, j: (i, j)),
                     pl.BlockSpec((1, 1, tn), lambda i, j: (segr(i), 0, j))]
        args += [r, gate]
    out_shape = [jax.ShapeDtypeStruct((m, n), out_dtype)]
    out_specs = [pl.BlockSpec((tm, tn), lambda i, j: (i, j))]
    if emit_ht:
        out_shape.append(jax.ShapeDtypeStruct((k, m), BF16))
        out_specs.append(pl.BlockSpec((k, tm), lambda i, j: (0, i)))
    t_from = None
    if t_cols:
        assert t_cols % tn == 0
        t_from = (n - t_cols) // tn
        out_shape.append(jax.ShapeDtypeStruct((t_cols, m), BF16))
        out_specs.append(pl.BlockSpec((tn, tm), lambda i, j: (jnp.maximum(j - t_from, 0), i)))
    scratch = [pltpu.VMEM((tm, k), BF16)] if norm is not None else []
    body = functools.partial(_mm_body, has_norm=norm is not None, emit_ht=emit_ht,
                             rope_tiles=rope_tiles, has_res=res is not None, t_from=t_from)
    outs = pl.pallas_call(
        body, grid=grid, in_specs=in_specs, out_specs=out_specs, out_shape=out_shape,
        scratch_shapes=scratch, compiler_params=_cparams(("parallel", "arbitrary")),
        name="fused_matmul")(*args)
    return outs if len(outs) > 1 else outs[0]


def _na_body(q_ref, k_ref, v_ref, kc_ref, vc_ref, bias_ref, o_ref, *, rows):
    lane = lax.broadcasted_iota(jnp.int32, (GRID_W, LANES), 1)
    head_masks = [lane < NA_HEAD_DIM, lane >= NA_HEAD_DIM]
    kc = kc_ref[...]
    vc = vc_ref[...]
    scale = NA_HEAD_DIM ** -0.5
    nt = (((1,), (1,)), ((), ()))

    def row_body(r, carry):
        r_start = jnp.clip(r - NA_WIN_H // 2, 0, rows - NA_WIN_H)
        off = r_start - r + NA_WIN_H - 1
        q = q_ref[pl.ds(pl.multiple_of(r * GRID_W, GRID_W), GRID_W), :]
        k_start = pl.multiple_of(r_start * GRID_W, GRID_W)
        kw = k_ref[pl.ds(k_start, NA_WIN_H * GRID_W), :]
        vw = v_ref[pl.ds(k_start, NA_WIN_H * GRID_W), :]
        outs = []
        for hh in range(2):
            qm = jnp.where(head_masks[hh], q, jnp.zeros_like(q))
            s_nb = lax.dot_general(qm, kw, nt, preferred_element_type=F32) * scale + bias_ref[off, hh]
            s_cx = lax.dot_general(qm, kc, nt, preferred_element_type=F32) * scale
            m = jnp.maximum(jnp.max(s_nb, axis=1, keepdims=True), jnp.max(s_cx, axis=1, keepdims=True))
            p_nb = jnp.exp(s_nb - m)
            p_cx = jnp.exp(s_cx - m)
            l = jnp.sum(p_nb, axis=1, keepdims=True) + jnp.sum(p_cx, axis=1, keepdims=True)
            o = (jnp.dot(p_nb.astype(BF16), vw, preferred_element_type=F32)
                 + jnp.dot(p_cx.astype(BF16), vc, preferred_element_type=F32))
            outs.append(o / l)
        o_ref[pl.ds(pl.multiple_of(r * GRID_W, GRID_W), GRID_W), :] = jnp.where(
            head_masks[0], outs[0], outs[1]).astype(o_ref.dtype)
        return carry

    lax.fori_loop(0, rows, row_body, 0)


def _na_bias_table(rpb):
    col = np.arange(GRID_W)
    col_start = np.clip(col - NA_WIN_W // 2, 0, GRID_W - NA_WIN_W)
    kc = np.arange(GRID_W)
    inside = (kc[None, :] >= col_start[:, None]) & (kc[None, :] < col_start[:, None] + NA_WIN_W)
    dc = np.clip(kc[None, :] - col[:, None] + NA_WIN_W - 1, 0, 2 * NA_WIN_W - 2)
    per_dr = jnp.where(inside[None, None], rpb[:, :, dc], NEG)
    tabs = []
    for off in range(NA_WIN_H):
        blk = per_dr[:, off:off + NA_WIN_H]
        tabs.append(blk.transpose(0, 2, 1, 3).reshape(NA_HEADS, GRID_W, NA_WIN_H * GRID_W))
    return jnp.stack(tabs)


def neighbourhood_attention(qkv, rpb, *, batch, seq, ctx_len):
    rows = seq // GRID_W
    bias = _na_bias_table(rpb)
    hp = NA_HEADS // 2
    ctx_blk0 = (batch * seq) // ctx_len
    body = functools.partial(_na_body, rows=rows)
    return pl.pallas_call(
        body, grid=(batch, hp),
        in_specs=[pl.BlockSpec((seq, LANES), lambda b, h: (b, h)),
                  pl.BlockSpec((seq, LANES), lambda b, h: (b, hp + h)),
                  pl.BlockSpec((seq, LANES), lambda b, h: (b, 2 * hp + h)),
                  pl.BlockSpec((ctx_len, LANES), lambda b, h: (ctx_blk0 + b, hp + h)),
                  pl.BlockSpec((ctx_len, LANES), lambda b, h: (ctx_blk0 + b, 2 * hp + h)),
                  pl.BlockSpec((NA_WIN_H, 2, GRID_W, NA_WIN_H * GRID_W), lambda b, h: (0, h, 0, 0))],
        out_specs=pl.BlockSpec((seq, LANES), lambda b, h: (b, h)),
        out_shape=jax.ShapeDtypeStruct((batch * seq, NA_WIDTH), BF16),
        compiler_params=_cparams(("parallel", "parallel")), name="neighbourhood_attention",
    )(qkv, qkv, qkv, qkv, qkv, bias)


def _ctx_attn_body(q_ref, k_ref, v_ref, o_ref):
    lane = lax.broadcasted_iota(jnp.int32, q_ref.shape, 1)
    q = q_ref[...]
    k = k_ref[...]
    v = v_ref[...]
    scale = NA_HEAD_DIM ** -0.5
    outs = []
    for hh in range(2):
        msk = (lane < NA_HEAD_DIM) if hh == 0 else (lane >= NA_HEAD_DIM)
        qm = jnp.where(msk, q, jnp.zeros_like(q))
        s = lax.dot_general(qm, k, (((1,), (1,)), ((), ())), preferred_element_type=F32) * scale
        m = jnp.max(s, axis=1, keepdims=True)
        p = jnp.exp(s - m)
        l = jnp.sum(p, axis=1, keepdims=True)
        outs.append(jnp.dot(p.astype(BF16), v, preferred_element_type=F32) / l)
    o_ref[...] = jnp.where(lane < NA_HEAD_DIM, outs[0], outs[1]).astype(o_ref.dtype)


def context_attention(qkv, *, batch, seq, ctx_len):
    hp = NA_HEADS // 2
    ctx_blk0 = (batch * seq) // ctx_len
    return pl.pallas_call(
        _ctx_attn_body, grid=(batch, hp),
        in_specs=[pl.BlockSpec((ctx_len, LANES), lambda b, h: (ctx_blk0 + b, h)),
                  pl.BlockSpec((ctx_len, LANES), lambda b, h: (ctx_blk0 + b, hp + h)),
                  pl.BlockSpec((ctx_len, LANES), lambda b, h: (ctx_blk0 + b, 2 * hp + h))],
        out_specs=pl.BlockSpec((ctx_len, LANES), lambda b, h: (b, h)),
        out_shape=jax.ShapeDtypeStruct((batch * ctx_len, NA_WIDTH), BF16),
        compiler_params=_cparams(("parallel", "parallel")), name="context_attention",
    )(qkv, qkv, qkv)


def _dft_cs(n):
    k = np.arange(n)
    ang = 2.0 * np.pi * ((k[:, None] * k[None, :]) % n) / n
    return np.cos(ang), np.sin(ang)


def _fourier_long_body(f_ref, wch_ref, ma_ref, mb_ref, o_ref, pr_ref, pi_ref, z_ref, *, n1, n2):
    c = FNET_GROUP_DIM
    p = jnp.dot(f_ref[...].astype(BF16), wch_ref[...], preferred_element_type=F32)
    pr_ref[...] = p[:, :c]
    pi_ref[...] = p[:, c:]

    def stage_a(j, carry):
        rows = pl.ds(j, n1, stride=n2)
        x = jnp.concatenate([pr_ref[rows, :], pi_ref[rows, :]], axis=1).astype(BF16)
        y = jnp.dot(ma_ref[j], x, preferred_element_type=F32)
        zr = y[:n1, :c] - y[n1:, c:]
        zi = y[n1:, :c] + y[:n1, c:]
        z_ref[pl.ds(j, n1, stride=2 * n2), :] = zr
        z_ref[pl.ds(n2 + j, n1, stride=2 * n2), :] = zi
        return carry

    lax.fori_loop(0, n2, stage_a, 0)

    def stage_b(k1, carry):
        z = z_ref[pl.ds(pl.multiple_of(k1 * 2 * n2, 2 * n2), 2 * n2), :].astype(BF16)
        y = jnp.dot(mb_ref[...], z, preferred_element_type=F32)
        o_ref[pl.ds(k1, n2, stride=n1), :] = y.astype(o_ref.dtype)
        return carry

    lax.fori_loop(0, n1, stage_b, 0)


def fourier_long(f, *, batch, seq):
    c = FNET_GROUP_DIM
    groups = f.shape[1] // c
    n2 = 128
    n1 = seq // n2
    norm = 1.0 / math.sqrt(seq * c)
    cc, sc = _dft_cs(c)
    wch = np.concatenate([cc, -sc], axis=1)
    c1, s1 = _dft_cs(n1)
    j = np.arange(n2)
    k1 = np.arange(n1)
    tw = 2.0 * np.pi * (j[:, None] * k1[None, :]) / seq
    ar = np.cos(tw)[:, :, None] * c1[None] - np.sin(tw)[:, :, None] * s1[None]
    ai = -(np.cos(tw)[:, :, None] * s1[None] + np.sin(tw)[:, :, None] * c1[None])
    ma = np.concatenate([ar, ai], axis=1)
    c2, s2 = _dft_cs(n2)
    mb = np.concatenate([c2, s2], axis=1) * norm
    body = functools.partial(_fourier_long_body, n1=n1, n2=n2)
    return pl.pallas_call(
        body, grid=(batch, groups),
        in_specs=[pl.BlockSpec((seq, c), lambda b, g: (b, g)),
                  pl.BlockSpec((c, 2 * c), lambda b, g: (0, 0)),
                  pl.BlockSpec((n2, 2 * n1, n1), lambda b, g: (0, 0, 0)),
                  pl.BlockSpec((n2, 2 * n2), lambda b, g: (0, 0))],
        out_specs=pl.BlockSpec((seq, c), lambda b, g: (b, g)),
        out_shape=jax.ShapeDtypeStruct((batch * seq, groups * c), F32),
        scratch_shapes=[pltpu.VMEM((seq, c), F32), pltpu.VMEM((seq, c), F32), pltpu.VMEM((2 * seq, c), F32)],
        compiler_params=_cparams(("parallel", "parallel")), name="fourier_long",
    )(f, jnp.asarray(wch, BF16), jnp.asarray(ma, BF16), jnp.asarray(mb, BF16))


def _fourier_short_body(f_ref, wch_ref, mp_ref, o_ref):
    c = FNET_GROUP_DIM
    p = jnp.dot(f_ref[...].astype(BF16), wch_ref[...], preferred_element_type=F32)
    z = jnp.concatenate([p[:, :c], p[:, c:]], axis=0).astype(BF16)
    o_ref[...] = jnp.dot(mp_ref[...], z, preferred_element_type=F32).astype(o_ref.dtype)


def fourier_short(f, *, batch, n, row_block0):
    c = FNET_GROUP_DIM
    groups = f.shape[1] // c
    norm = 1.0 / math.sqrt(n * c)
    cc, sc = _dft_cs(c)
    wch = np.concatenate([cc, -sc], axis=1)
    cn, sn = _dft_cs(n)
    mp = np.concatenate([cn, sn], axis=1) * norm
    return pl.pallas_call(
        _fourier_short_body, grid=(batch, groups),
        in_specs=[pl.BlockSpec((n, c), lambda b, g: (row_block0 + b, g)),
                  pl.BlockSpec((c, 2 * c), lambda b, g: (0, 0)),
                  pl.BlockSpec((n, 2 * n), lambda b, g: (0, 0))],
        out_specs=pl.BlockSpec((n, c), lambda b, g: (b, g)),
        out_shape=jax.ShapeDtypeStruct((batch * n, groups * c), F32),
        compiler_params=_cparams(("parallel", "parallel")), name="fourier_short",
    )(f, jnp.asarray(wch, BF16), jnp.asarray(mp, BF16))


LOG2E = 1.4426950408889634


ONES_ROWS = 16


def _diff_body(lam_ref, q_ref, kl_ref, vtl_ref, kc_ref, vtc_ref, g_ref, o_ref,
               qs_ref, m_ref, acc_ref, *, nk, out_scale):
    j = pl.program_id(3)
    d = DIFF_HEAD_DIM

    @pl.when(j == 0)
    def _():
        q = q_ref[...].astype(F32) * (d ** -0.5 * LOG2E)
        lane = lax.broadcasted_iota(jnp.int32, q.shape, 1)
        qs_ref[0] = jnp.where(lane < d, q, 0.0).astype(BF16)
        qs_ref[1] = jnp.where(lane >= d, q, 0.0).astype(BF16)
        m_ref[...] = jnp.full(m_ref.shape, NEG, F32)
        acc_ref[...] = jnp.zeros(acc_ref.shape, F32)

    def step(k, vt):
        sts = [lax.dot_general(k, qs_ref[comp], (((1,), (1,)), ((), ())), preferred_element_type=F32)
               for comp in range(2)]
        for comp in range(2):
            st = sts[comp]
            m_old = m_ref[comp]
            m_new = jnp.maximum(m_old, jnp.max(st, axis=0, keepdims=True))
            alpha = jnp.exp2(m_old - m_new)
            p = jnp.exp2((st - m_new).astype(BF16))
            acc_ref[comp] = alpha * acc_ref[comp] + jnp.dot(vt, p, preferred_element_type=F32)
            m_ref[comp] = m_new

    @pl.when(j < nk)
    def _():
        step(kl_ref[...], vtl_ref[...])

    @pl.when(j == nk)
    def _():
        step(kc_ref[...], vtc_ref[...])
        lam = lam_ref[0]
        n = 2 * d
        o0 = acc_ref[0, :n, :] / acc_ref[0, n:n + 1, :]
        o1 = acc_ref[1, :n, :] / acc_ref[1, n:n + 1, :]
        ot = o0 - lam * o1
        yt = ot * lax.rsqrt(jnp.mean(ot * ot, axis=0, keepdims=True) + EPS)
        o_ref[...] = (yt.T * g_ref[...] * out_scale).astype(o_ref.dtype)


def diff_attention(qkv, vt, lam, subln_g, lambda_init, *, batch, seq, ctx_len, tq=512, tk=1024):
    h = DIFF_HEADS
    nq = seq // tq
    nk = seq // tk
    vrows = 2 * DIFF_HEAD_DIM + ONES_ROWS
    ctx_blk0 = (batch * seq) // ctx_len
    body = functools.partial(_diff_body, nk=nk, out_scale=1.0 - lambda_init)
    lat = lambda b, hh, i, j: b * nk + jnp.minimum(j, nk - 1)
    return pl.pallas_call(
        body, grid=(batch, h, nq, nk + 1),
        in_specs=[pl.BlockSpec(memory_space=pltpu.SMEM),
                  pl.BlockSpec((tq, LANES), lambda b, hh, i, j: (b * nq + i, hh)),
                  pl.BlockSpec((tk, LANES), lambda b, hh, i, j: (lat(b, hh, i, j), h + hh)),
                  pl.BlockSpec((vrows, tk), lambda b, hh, i, j: (hh, lat(b, hh, i, j))),
                  pl.BlockSpec((ctx_len, LANES), lambda b, hh, i, j: (ctx_blk0 + b, h + hh)),
                  pl.BlockSpec((vrows, ctx_len), lambda b, hh, i, j: (hh, ctx_blk0 + b)),
                  pl.BlockSpec((1, LANES), lambda b, hh, i, j: (0, 0))],
        out_specs=pl.BlockSpec((tq, LANES), lambda b, hh, i, j: (b * nq + i, hh)),
        out_shape=jax.ShapeDtypeStruct((batch * seq, h * LANES), BF16),
        scratch_shapes=[pltpu.VMEM((2, tq, LANES), BF16), pltpu.VMEM((2, 1, tq), F32),
                        pltpu.VMEM((2, vrows, tq), F32)],
        compiler_params=_cparams(("parallel", "parallel", "parallel", "arbitrary")),
        name="diff_attention",
    )(lam, qkv, qkv, vt, qkv, vt, subln_g)


def _with_ones_rows(vt, heads):
    hw, rows = vt.shape
    blocks = vt.reshape(heads, hw // heads, rows)
    ones = jnp.ones((heads, ONES_ROWS, rows), vt.dtype)
    return jnp.concatenate([blocks, ones], axis=1).reshape(hw + heads * ONES_ROWS, rows)


def _extract16(s, pos):
    big = jnp.int32(2 ** 30)
    rank = jnp.full(s.shape, float(PEER_TOPK), F32)
    vals = []
    for r in range(PEER_TOPK):
        m = jnp.max(s, axis=0, keepdims=True)
        sel = jnp.min(jnp.where(s == m, pos, big), axis=0, keepdims=True)
        hit = pos == sel
        vals.append(m)
        rank = jnp.where(hit, float(r), rank)
        s = jnp.where(hit, -jnp.inf, s)
    return vals, rank, rank < float(PEER_TOPK)


_CAND_GROUPS = ((0, 16), (1, 8), (2, 5), (3, 4), (4, 3), (5, 2), (6, 2), (7, 2))


def _peer_topk_body(q_ref, keys_ref, r1_ref, e1_ref, n0_ref, c0_ref):
    q = q_ref[...]
    kd = q.shape[1] // 2
    tb = q.shape[0]
    res = []
    for half in range(2):
        s = lax.dot_general(keys_ref[0, half], q[:, half * kd:(half + 1) * kd],
                            (((1,), (1,)), ((), ())), preferred_element_type=F32)
        pos = lax.broadcasted_iota(jnp.int32, s.shape, 0)
        vals, rank, _ = _extract16(s, pos)
        res.append((s, vals, rank))
    (s0, v0, rank0), (s1, v1, rank1) = res
    v1_lo = jnp.concatenate(v1[:SUBLANES], axis=0)
    v1_hi = jnp.concatenate(v1[SUBLANES:], axis=0)
    row8 = lax.broadcasted_iota(jnp.int32, (SUBLANES, tb), 0)
    pieces, poses = [], []
    for a, nb in _CAND_GROUPS:
        if nb > SUBLANES:
            pieces += [v0[a] + v1_lo, v0[a] + v1_hi]
            poses += [a * PEER_TOPK + row8, a * PEER_TOPK + SUBLANES + row8]
        else:
            pieces.append(jnp.where(row8 < nb, v0[a] + v1_lo, -jnp.inf))
            poses.append(a * PEER_TOPK + row8)
    pieces.append(jnp.concatenate(v0[SUBLANES:], axis=0) + v1[0])
    poses.append((row8 + SUBLANES) * PEER_TOPK)
    cand = jnp.concatenate(pieces, axis=0)
    cpos = jnp.concatenate(poses, axis=0)
    best, _, taken = _extract16(cand, cpos)
    z = sum(jnp.exp(b - best[0]) for b in best)
    takenf = taken.astype(F32)
    counts = []
    row = 0
    for a, nb in _CAND_GROUPS:
        rows = 2 * SUBLANES if nb > SUBLANES else SUBLANES
        counts.append(jnp.sum(takenf[row:row + rows], axis=0, keepdims=True))
        row += rows
    counts += [takenf[row + i:row + i + 1] for i in range(SUBLANES)]
    n0 = jnp.zeros(s0.shape, F32)
    for a in range(PEER_TOPK):
        n0 = jnp.where(rank0 == float(a), counts[a], n0)
    r1_ref[0] = rank1.astype(r1_ref.dtype)
    e1_ref[0] = jnp.exp(s1 - v1[0]).astype(e1_ref.dtype)
    n0_ref[0] = n0
    c0_ref[0] = jnp.exp(s0 - v0[0]) / z


def peer_topk(q, sub_keys, *, tb=256):
    t = q.shape[0]
    h = PEER_HEADS
    spec = pl.BlockSpec((1, PEER_N_KEYS, tb), lambda i, hh: (hh, 0, i))
    shapes = [jax.ShapeDtypeStruct((h, PEER_N_KEYS, t), dt) for dt in (BF16, BF16, F32, F32)]
    return pl.pallas_call(
        _peer_topk_body, grid=(t // tb, h),
        in_specs=[pl.BlockSpec((tb, 2 * LANES), lambda i, hh: (i, hh)),
                  pl.BlockSpec((1, 2, PEER_N_KEYS, LANES), lambda i, hh: (hh, 0, 0, 0))],
        out_specs=[spec] * 4, out_shape=shapes,
        compiler_params=_cparams(("parallel", "parallel")), name="peer_topk",
    )(q, sub_keys)


PEER_PIPE_LAG = 2
GELU_C1 = math.sqrt(2.0 / math.pi)
GELU_C2 = 0.044715 * GELU_C1


def _peer_mlp_body(ht_ref, u_ref, vt_ref, r1_ref, e1_ref, n0_ref, c0_ref, x_ref, gate_ref, o_ref,
                   at_ref, wt_ref, acc_ref, *, blocks_per_step, n_chunks, n_flat):
    s = pl.program_id(0)
    cur = s % 2
    prev = 1 - cur
    c1 = jnp.clip(s - 1, 0, n_flat - 1) % n_chunks
    c2 = jnp.clip(s - 2, 0, n_flat - 1) % n_chunks

    @pl.when(s == 0)
    def _():
        at_ref[...] = jnp.zeros(at_ref.shape, F32)
        wt_ref[...] = jnp.zeros(wt_ref.shape, BF16)

    @pl.when(c2 == 0)
    def _():
        acc_ref[...] = jnp.zeros(acc_ref.shape, F32)

    acc_ref[...] += jnp.dot(vt_ref[...], wt_ref[prev], preferred_element_type=F32)

    tiles = []
    for ib in range(blocks_per_step):
        i = c1 * blocks_per_step + ib
        g = None
        for h in range(PEER_HEADS):
            n0 = n0_ref[h, pl.ds(i, 1), :].astype(BF16)
            c0 = c0_ref[h, pl.ds(i, 1), :].astype(BF16)
            term = jnp.where(r1_ref[h] < n0, e1_ref[h], jnp.zeros((), BF16)) * c0
            g = term if g is None else g + term
        tiles.append(g)
    gt = jnp.concatenate(tiles, axis=0)
    a = at_ref[prev].astype(BF16)
    inner = a * (GELU_C1 + GELU_C2 * (a * a))
    half = 0.5 * a
    wt_ref[cur] = (half + half * jnp.tanh(inner)) * gt

    at_ref[cur] = jnp.dot(u_ref[...], ht_ref[...], preferred_element_type=F32)

    @pl.when((c2 == n_chunks - 1) & (s >= PEER_PIPE_LAG))
    def _():
        o_ref[...] = x_ref[...] + gate_ref[0] * acc_ref[...].T


def peer_experts(x, ht, routing, u, vt, g_f, *, tm=512, te=1024, seg_rows=8192):
    t, d = x.shape
    n_exp = u.shape[0]
    n_seg = g_f.shape[0]
    n_chunks = n_exp // te
    n_flat = (t // tm) * n_chunks
    seg = functools.partial(_seg_of_block, tm=tm, seg_rows=seg_rows, n_seg=n_seg)

    def lagged(lag):
        f = lambda s: jnp.clip(s - lag, 0, n_flat - 1)
        return (lambda s: f(s) // n_chunks), (lambda s: f(s) % n_chunks)

    (blk0, chunk0), (blk1, _), (blk2, chunk2) = lagged(0), lagged(1), lagged(2)
    route_spec = pl.BlockSpec((PEER_HEADS, PEER_N_KEYS, tm), lambda s: (0, 0, blk1(s)))
    body = functools.partial(_peer_mlp_body, blocks_per_step=te // PEER_N_KEYS, n_chunks=n_chunks, n_flat=n_flat)
    return pl.pallas_call(
        body, grid=(n_flat + PEER_PIPE_LAG,),
        in_specs=[pl.BlockSpec((d, tm), lambda s: (0, blk0(s))),
                  pl.BlockSpec((te, d), lambda s: (chunk0(s), 0)),
                  pl.BlockSpec((d, te), lambda s: (0, chunk2(s))),
                  route_spec, route_spec, route_spec, route_spec,
                  pl.BlockSpec((tm, d), lambda s: (blk2(s), 0)),
                  pl.BlockSpec((1, 1, d), lambda s: (seg(blk2(s)), 0, 0))],
        out_specs=pl.BlockSpec((tm, d), lambda s: (blk2(s), 0)),
        out_shape=jax.ShapeDtypeStruct((t, d), F32),
        scratch_shapes=[pltpu.VMEM((2, te, tm), F32), pltpu.VMEM((2, te, tm), BF16), pltpu.VMEM((d, tm), F32)],
        compiler_params=_cparams(("arbitrary",)), name="peer_experts",
    )(ht, u, vt, *routing, x, g_f)


def _rms_body(x_ref, g_ref, o_ref):
    xf = x_ref[...]
    o_ref[...] = xf * lax.rsqrt(jnp.mean(xf * xf, axis=-1, keepdims=True) + EPS) * g_ref[...]


def final_rms_norm(x, g, *, tm=1024):
    m, d = x.shape
    return pl.pallas_call(
        _rms_body, grid=(m // tm,),
        in_specs=[pl.BlockSpec((tm, d), lambda i: (i, 0)), pl.BlockSpec((1, d), lambda i: (0, 0))],
        out_specs=pl.BlockSpec((tm, d), lambda i: (i, 0)),
        out_shape=jax.ShapeDtypeStruct((m, d), F32),
        compiler_params=_cparams(("parallel",)), name="final_rms_norm")(x, g)


def _rope_tables(seq, extra_rows):
    quarter = DIFF_HEAD_DIM // 4
    t = jnp.arange(seq)
    row = (t // GRID_W).astype(F32)
    col = (t % GRID_W).astype(F32)
    inv = ROPE_BASE ** (-jnp.arange(quarter, dtype=F32) / quarter)
    cos_r, sin_r = jnp.cos(row[:, None] * inv), jnp.sin(row[:, None] * inv)
    cos_c, sin_c = jnp.cos(col[:, None] * inv), jnp.sin(col[:, None] * inv)
    cos64 = jnp.concatenate([cos_r, cos_r, cos_c, cos_c], axis=1)
    sin64 = jnp.concatenate([-sin_r, sin_r, -sin_c, sin_c], axis=1)
    cos = jnp.concatenate([cos64, cos64], axis=1)
    sin = jnp.concatenate([sin64, sin64], axis=1)
    cos = jnp.concatenate([cos, jnp.ones((extra_rows, LANES), F32)], axis=0)
    sin = jnp.concatenate([sin, jnp.zeros((extra_rows, LANES), F32)], axis=0)
    return cos, sin


def kernel(x, c, ctx, c_ctx, w_mod, b_mod, norm_mix_g, norm_ffn_g, even_w_in, even_w_out, na_rpb,
           odd_w_in, odd_w_out, diff_lambda_q1, diff_lambda_k1, diff_lambda_q2, diff_lambda_k2,
           diff_subln_g, peer_w_q, peer_sub_keys, peer_u, peer_v, final_norm_g):
    batch, seq, d = x.shape
    ctx_len = ctx.shape[1]
    depth = w_mod.shape[0]
    n_lat = batch * seq
    tm = 512
    xs = jnp.concatenate([x.reshape(n_lat, d), ctx.reshape(batch * ctx_len, d)], axis=0)
    cond = jnp.concatenate([c, c_ctx[None], jnp.zeros((SUBLANES - batch - 1, d), F32)], axis=0)
    cond = jax.nn.silu(cond)
    n_seg = batch + 1

    for layer in range(depth):
        last = layer == depth - 1
        jl = layer // 2
        mod = fused_matmul(cond, w_mod[layer].astype(BF16), tm=SUBLANES, tn=1024)[:n_seg] + b_mod[layer]
        sh_m, sc_m, g_m, sh_f, sc_f, g_f = [m_[:, None, :] for m_ in jnp.split(mod, 6, axis=-1)]
        norm_mix = (norm_mix_g[layer][None], sh_m, sc_m)
        norm_ffn = (norm_ffn_g[layer][None], sh_f, sc_f)
        if layer % 2 == 0:
            w_in = even_w_in[jl].astype(BF16)
            w_out = even_w_out[jl].astype(BF16)
            qkv = fused_matmul(xs, w_in[:, :3 * NA_WIDTH], norm=norm_mix, out_dtype=BF16, tm=tm)
            f = fused_matmul(xs, w_in[:, 3 * NA_WIDTH:], norm=norm_mix, tm=tm)
            attn = neighbourhood_attention(qkv, na_rpb[jl], batch=batch, seq=seq, ctx_len=ctx_len)
            four = fourier_long(f, batch=batch, seq=seq)
            if not last:
                attn = jnp.concatenate([attn, context_attention(qkv, batch=batch, seq=seq, ctx_len=ctx_len)], axis=0)
                four = jnp.concatenate([four, fourier_short(f, batch=batch, n=ctx_len, row_block0=n_lat // ctx_len)], axis=0)
            rows = attn.shape[0]
            xs_mix = fused_matmul(attn, w_out[:NA_WIDTH], res=(xs[:rows], g_m), tm=tm)
            xs_mix = fused_matmul(four, w_out[NA_WIDTH:], res=(xs_mix, g_m), tm=tm)
        else:
            lambda_init = 0.8 - 0.6 * math.exp(-0.3 * layer)
            lam = (jnp.exp(jnp.sum(diff_lambda_q1[jl] * diff_lambda_k1[jl]))
                   - jnp.exp(jnp.sum(diff_lambda_q2[jl] * diff_lambda_k2[jl])) + lambda_init).reshape(1)
            cos, sin = _rope_tables(seq, tm)
            v_width = DIFF_HEADS * 2 * DIFF_HEAD_DIM
            qkv, vt = fused_matmul(xs, odd_w_in[jl].astype(BF16), norm=norm_mix, out_dtype=BF16, tm=tm,
                                   rope=(cos, sin, 2 * DIFF_HEADS * 2 * DIFF_HEAD_DIM, seq // tm, n_lat // tm),
                                   t_cols=v_width)
            o = diff_attention(qkv, _with_ones_rows(vt, DIFF_HEADS), lam, diff_subln_g[jl][None], lambda_init,
                               batch=batch, seq=seq, ctx_len=ctx_len)
            if not last:
                raise NotImplementedError("context output of a differential-attention layer")
            xs_mix = fused_matmul(o, odd_w_out[jl].astype(BF16), res=(xs[:n_lat], g_m), tm=tm)
        q, ht = fused_matmul(xs_mix, peer_w_q[layer].astype(BF16), norm=norm_ffn, emit_ht=True, tm=tm)
        routing = peer_topk(q, peer_sub_keys[layer])
        xs = peer_experts(xs_mix, ht, routing, peer_u[layer].astype(BF16), peer_v[layer].astype(BF16).T, g_f)
        if not last and xs.shape[0] == n_lat:
            raise NotImplementedError("context stream dropped before the last layer")
    out = final_rms_norm(xs[:n_lat], final_norm_g[None])
    return out.reshape(batch, seq, d)
```

```python
import functools
import math

import jax
import jax.numpy as jnp
import numpy as np
from jax import lax
from jax.experimental import pallas as pl
from jax.experimental.pallas import tpu as pltpu

F32 = jnp.float32
BF16 = jnp.bfloat16

D_MODEL = 1024
GRID_W = 64
EPS = 1e-6
ROPE_BASE = 10000.0
NA_HEADS = 8
NA_HEAD_DIM = 64
NA_WIDTH = 512
NA_WIN_H = 8
NA_WIN_W = 16
FNET_GROUP_DIM = 128
DIFF_HEADS = 8
DIFF_HEAD_DIM = 64
PEER_HEADS = 8
PEER_N_KEYS = 128
PEER_TOPK = 16
LANES = 128
SUBLANES = 8
NEG = -1e30
VMEM_LIMIT = 56 * 1024 * 1024


def _cparams(sem, vmem=VMEM_LIMIT):
    return pltpu.CompilerParams(dimension_semantics=sem, vmem_limit_bytes=vmem)


def _mm_body(*refs, has_norm, emit_ht, rope_tiles, has_res, t_from):
    it = iter(refs)
    x_ref = next(it)
    if has_norm:
        g_ref, sh_ref, sc_ref = next(it), next(it), next(it)
    w_ref = next(it)
    if rope_tiles:
        cos_ref, sin_ref = next(it), next(it)
    if has_res:
        res_ref, gate_ref = next(it), next(it)
    o_ref = next(it)
    if emit_ht:
        ht_ref = next(it)
    if t_from is not None:
        ot_ref = next(it)
    if has_norm:
        xs_ref = next(it)
    j = pl.program_id(1)

    if has_norm:
        @pl.when(j == 0)
        def _():
            xf = x_ref[...]
            y = xf * lax.rsqrt(jnp.mean(xf * xf, axis=-1, keepdims=True) + EPS)
            y = y * g_ref[...] * (1.0 + sc_ref[0]) + sh_ref[0]
            xs_ref[...] = y.astype(BF16)
            if emit_ht:
                ht_ref[...] = y.T.astype(BF16)
        a = xs_ref[...]
    else:
        a = x_ref[...].astype(BF16)
    acc = jnp.dot(a, w_ref[...], preferred_element_type=F32)

    if t_from is not None:
        @pl.when(j >= t_from)
        def _():
            ot_ref[...] = acc.T.astype(ot_ref.dtype)

    def finish(v):
        if has_res:
            v = res_ref[...] + gate_ref[0] * v
        o_ref[...] = v.astype(o_ref.dtype)

    if rope_tiles:
        @pl.when(j < rope_tiles)
        def _():
            tn = acc.shape[1]
            reps = tn // LANES
            c = jnp.concatenate([cos_ref[...]] * reps, axis=1)
            s = jnp.concatenate([sin_ref[...]] * reps, axis=1)
            lane = lax.broadcasted_iota(jnp.int32, acc.shape, 1)
            first = (lane % 32) < 16
            partner = jnp.where(first, pltpu.roll(acc, tn - 16, 1), pltpu.roll(acc, 16, 1))
            finish(acc * c + partner * s)

        @pl.when(j >= rope_tiles)
        def _():
            finish(acc)
    else:
        finish(acc)


def _seg_of_block(i, tm, seg_rows, n_seg):
    return jnp.minimum((i * tm) // seg_rows, n_seg - 1)


def fused_matmul(x, w, *, norm=None, rope=None, res=None, emit_ht=False, t_cols=0, out_dtype=F32,
                 tm=512, tn=512, seg_rows=8192):
    m, k = x.shape
    n = w.shape[1]
    tm = min(tm, m)
    tn = min(tn, n)
    assert m % tm == 0 and n % tn == 0
    grid = (m // tm, n // tn)
    in_specs = [pl.BlockSpec((tm, k), lambda i, j: (i, 0))]
    args = [x]
    n_seg = 1
    if norm is not None:
        g, sh, sc = norm
        n_seg = sh.shape[0]
        seg = functools.partial(_seg_of_block, tm=tm, seg_rows=seg_rows, n_seg=n_seg)
        in_specs += [pl.BlockSpec((1, k), lambda i, j: (0, 0)),
                     pl.BlockSpec((1, 1, k), lambda i, j: (seg(i), 0, 0)),
                     pl.BlockSpec((1, 1, k), lambda i, j: (seg(i), 0, 0))]
        args += [g, sh, sc]
    in_specs.append(pl.BlockSpec((k, tn), lambda i, j: (0, j)))
    args.append(w)
    rope_tiles = 0
    if rope is not None:
        cos, sin, n_rope_cols, n_pos_blocks, n_lat_blocks = rope
        rope_tiles = n_rope_cols // tn
        pos = lambda i, j: (jnp.where(i < n_lat_blocks, i % n_pos_blocks, n_pos_blocks), 0)
        in_specs += [pl.BlockSpec((tm, LANES), pos), pl.BlockSpec((tm, LANES), pos)]
        args += [cos, sin]
    if res is not None:
        r, gate = res
        n_seg_r = gate.shape[0]
        segr = functools.partial(_seg_of_block, tm=tm, seg_rows=seg_rows, n_seg=n_seg_r)
        in_specs += [pl.BlockSpec((tm, tn), lambda i, j: (i, j)),
                     pl.BlockSpec((1, 1, tn), lambda i, j: (segr(i), 0, j))]
        args += [r, gate]
    out_shape = [jax.ShapeDtypeStruct((m, n), out_dtype)]
    out_specs = [pl.BlockSpec((tm, tn), lambda i, j: (i, j))]
    if emit_ht:
        out_shape.append(jax.ShapeDtypeStruct((k, m), BF16))
        out_specs.append(pl.BlockSpec((k, tm), lambda i, j: (0, i)))
    t_from = None
    if t_cols:
        assert t_cols % tn == 0
        t_from = (n - t_cols) // tn
        out_shape.append(jax.ShapeDtypeStruct((t_cols, m), BF16))
        out_specs.append(pl.BlockSpec((tn, tm), lambda i, j: (jnp.maximum(j - t_from, 0), i)))
    scratch = [pltpu.VMEM((tm, k), BF16)] if norm is not None else []
    body = functools.partial(_mm_body, has_norm=norm is not None, emit_ht=emit_ht,
                             rope_tiles=rope_tiles, has_res=res is not None, t_from=t_from)
    outs = pl.pallas_call(
        body, grid=grid, in_specs=in_specs, out_specs=out_specs, out_shape=out_shape,
        scratch_shapes=scratch, compiler_params=_cparams(("parallel", "arbitrary")),
        name="fused_matmul")(*args)
    return outs if len(outs) > 1 else outs[0]


NA_GROUP = 4
NA_UNION = NA_WIN_H + NA_GROUP


def _na_body(q_ref, k_ref, v_ref, kc_ref, vc_ref, bias_ref, o_ref, *, rows):
    nq = NA_GROUP * GRID_W
    nkeys = NA_UNION * GRID_W
    lane = lax.broadcasted_iota(jnp.int32, (nq, LANES), 1)
    head_masks = [lane < NA_HEAD_DIM, lane >= NA_HEAD_DIM]
    kc = kc_ref[...]
    vc = vc_ref[...]
    scale = NA_HEAD_DIM ** -0.5
    nt = (((1,), (1,)), ((), ()))
    n_groups = rows // NA_GROUP

    def group_body(g, carry):
        r0 = g * NA_GROUP
        u0 = jnp.clip(r0 - NA_WIN_H // 2, 0, rows - NA_UNION)
        pattern = jnp.where(g == 0, 0, jnp.where(g == n_groups - 1, 2, 1))
        q_rows = pl.ds(pl.multiple_of(r0 * GRID_W, nq), nq)
        q = q_ref[q_rows, :]
        k_start = pl.multiple_of(u0 * GRID_W, NA_GROUP * GRID_W)
        kw = k_ref[pl.ds(k_start, nkeys), :]
        vw = v_ref[pl.ds(k_start, nkeys), :]
        outs = []
        for hh in range(2):
            qm = jnp.where(head_masks[hh], q, jnp.zeros_like(q))
            bias = bias_ref[pattern, :, hh].reshape(nq, nkeys)
            s_nb = lax.dot_general(qm, kw, nt, preferred_element_type=F32) * scale + bias
            s_cx = lax.dot_general(qm, kc, nt, preferred_element_type=F32) * scale
            m = jnp.maximum(jnp.max(s_nb, axis=1, keepdims=True), jnp.max(s_cx, axis=1, keepdims=True))
            p_nb = jnp.exp(s_nb - m)
            p_cx = jnp.exp(s_cx - m)
            l = jnp.sum(p_nb, axis=1, keepdims=True) + jnp.sum(p_cx, axis=1, keepdims=True)
            o = (jnp.dot(p_nb.astype(BF16), vw, preferred_element_type=F32)
                 + jnp.dot(p_cx.astype(BF16), vc, preferred_element_type=F32))
            outs.append(o / l)
        o_ref[q_rows, :] = jnp.where(head_masks[0], outs[0], outs[1]).astype(o_ref.dtype)
        return carry

    lax.fori_loop(0, n_groups, group_body, 0)


def _na_bias_table(rpb):
    col = np.arange(GRID_W)
    col_start = np.clip(col - NA_WIN_W // 2, 0, GRID_W - NA_WIN_W)
    kc = np.arange(GRID_W)
    inside = (kc[None, :] >= col_start[:, None]) & (kc[None, :] < col_start[:, None] + NA_WIN_W)
    dc = np.clip(kc[None, :] - col[:, None] + NA_WIN_W - 1, 0, 2 * NA_WIN_W - 2)
    per_dr = jnp.where(inside[None, None], rpb[:, :, dc], NEG)
    masked = jnp.full((NA_HEADS, GRID_W, GRID_W), NEG, F32)
    half = NA_WIN_H // 2
    patterns = [(0, lambda dr: -dr), (-half, lambda dr: -half), (-NA_WIN_H, lambda dr: -half - dr)]
    tabs = []
    for union_off, win_start in patterns:
        per_row = []
        for dr in range(NA_GROUP):
            blocks = []
            for kk in range(NA_UNION):
                delta = union_off + kk - dr
                ok = win_start(dr) <= delta < win_start(dr) + NA_WIN_H
                blocks.append(per_dr[:, delta + NA_WIN_H - 1] if ok else masked)
            per_row.append(jnp.concatenate(blocks, axis=-1))
        tabs.append(jnp.stack(per_row))
    return jnp.stack(tabs)


def neighbourhood_attention(qkv, rpb, *, batch, seq, ctx_len):
    rows = seq // GRID_W
    assert rows % NA_GROUP == 0 and rows >= NA_UNION + NA_GROUP
    bias = _na_bias_table(rpb)
    hp = NA_HEADS // 2
    ctx_blk0 = (batch * seq) // ctx_len
    body = functools.partial(_na_body, rows=rows)
    return pl.pallas_call(
        body, grid=(batch, hp),
        in_specs=[pl.BlockSpec((seq, LANES), lambda b, h: (b, h)),
                  pl.BlockSpec((seq, LANES), lambda b, h: (b, hp + h)),
                  pl.BlockSpec((seq, LANES), lambda b, h: (b, 2 * hp + h)),
                  pl.BlockSpec((ctx_len, LANES), lambda b, h: (ctx_blk0 + b, hp + h)),
                  pl.BlockSpec((ctx_len, LANES), lambda b, h: (ctx_blk0 + b, 2 * hp + h)),
                  pl.BlockSpec((3, NA_GROUP, 2, GRID_W, NA_UNION * GRID_W), lambda b, h: (0, 0, h, 0, 0))],
        out_specs=pl.BlockSpec((seq, LANES), lambda b, h: (b, h)),
        out_shape=jax.ShapeDtypeStruct((batch * seq, NA_WIDTH), BF16),
        compiler_params=_cparams(("parallel", "parallel")), name="neighbourhood_attention",
    )(qkv, qkv, qkv, qkv, qkv, bias)


def _ctx_attn_body(q_ref, k_ref, v_ref, o_ref):
    lane = lax.broadcasted_iota(jnp.int32, q_ref.shape, 1)
    q = q_ref[...]
    k = k_ref[...]
    v = v_ref[...]
    scale = NA_HEAD_DIM ** -0.5
    outs = []
    for hh in range(2):
        msk = (lane < NA_HEAD_DIM) if hh == 0 else (lane >= NA_HEAD_DIM)
        qm = jnp.where(msk, q, jnp.zeros_like(q))
        s = lax.dot_general(qm, k, (((1,), (1,)), ((), ())), preferred_element_type=F32) * scale
        m = jnp.max(s, axis=1, keepdims=True)
        p = jnp.exp(s - m)
        l = jnp.sum(p, axis=1, keepdims=True)
        outs.append(jnp.dot(p.astype(BF16), v, preferred_element_type=F32) / l)
    o_ref[...] = jnp.where(lane < NA_HEAD_DIM, outs[0], outs[1]).astype(o_ref.dtype)


def context_attention(qkv, *, batch, seq, ctx_len):
    hp = NA_HEADS // 2
    ctx_blk0 = (batch * seq) // ctx_len
    return pl.pallas_call(
        _ctx_attn_body, grid=(batch, hp),
        in_specs=[pl.BlockSpec((ctx_len, LANES), lambda b, h: (ctx_blk0 + b, h)),
                  pl.BlockSpec((ctx_len, LANES), lambda b, h: (ctx_blk0 + b, hp + h)),
                  pl.BlockSpec((ctx_len, LANES), lambda b, h: (ctx_blk0 + b, 2 * hp + h))],
        out_specs=pl.BlockSpec((ctx_len, LANES), lambda b, h: (b, h)),
        out_shape=jax.ShapeDtypeStruct((batch * ctx_len, NA_WIDTH), BF16),
        compiler_params=_cparams(("parallel", "parallel")), name="context_attention",
    )(qkv, qkv, qkv)


def _dft_cs(n):
    k = np.arange(n)
    ang = 2.0 * np.pi * ((k[:, None] * k[None, :]) % n) / n
    return np.cos(ang), np.sin(ang)


def _fourier_long_body(f_ref, wch_ref, ma_ref, mb_ref, o_ref, pr_ref, pi_ref, z_ref, *, n1, n2):
    c = FNET_GROUP_DIM
    p = jnp.dot(f_ref[...].astype(BF16), wch_ref[...], preferred_element_type=F32)
    pr_ref[...] = p[:, :c]
    pi_ref[...] = p[:, c:]

    def stage_a(j, carry):
        rows = pl.ds(j, n1, stride=n2)
        x = jnp.concatenate([pr_ref[rows, :], pi_ref[rows, :]], axis=1).astype(BF16)
        y = jnp.dot(ma_ref[j], x, preferred_element_type=F32)
        zr = y[:n1, :c] - y[n1:, c:]
        zi = y[n1:, :c] + y[:n1, c:]
        z_ref[pl.ds(j, n1, stride=2 * n2), :] = zr
        z_ref[pl.ds(n2 + j, n1, stride=2 * n2), :] = zi
        return carry

    lax.fori_loop(0, n2, stage_a, 0)

    def stage_b(k1, carry):
        z = z_ref[pl.ds(pl.multiple_of(k1 * 2 * n2, 2 * n2), 2 * n2), :].astype(BF16)
        y = jnp.dot(mb_ref[...], z, preferred_element_type=F32)
        o_ref[pl.ds(k1, n2, stride=n1), :] = y.astype(o_ref.dtype)
        return carry

    lax.fori_loop(0, n1, stage_b, 0)


def fourier_long(f, *, batch, seq):
    c = FNET_GROUP_DIM
    groups = f.shape[1] // c
    n2 = 128
    n1 = seq // n2
    norm = 1.0 / math.sqrt(seq * c)
    cc, sc = _dft_cs(c)
    wch = np.concatenate([cc, -sc], axis=1)
    c1, s1 = _dft_cs(n1)
    j = np.arange(n2)
    k1 = np.arange(n1)
    tw = 2.0 * np.pi * (j[:, None] * k1[None, :]) / seq
    ar = np.cos(tw)[:, :, None] * c1[None] - np.sin(tw)[:, :, None] * s1[None]
    ai = -(np.cos(tw)[:, :, None] * s1[None] + np.sin(tw)[:, :, None] * c1[None])
    ma = np.concatenate([ar, ai], axis=1)
    c2, s2 = _dft_cs(n2)
    mb = np.concatenate([c2, s2], axis=1) * norm
    body = functools.partial(_fourier_long_body, n1=n1, n2=n2)
    return pl.pallas_call(
        body, grid=(batch, groups),
        in_specs=[pl.BlockSpec((seq, c), lambda b, g: (b, g)),
                  pl.BlockSpec((c, 2 * c), lambda b, g: (0, 0)),
                  pl.BlockSpec((n2, 2 * n1, n1), lambda b, g: (0, 0, 0)),
                  pl.BlockSpec((n2, 2 * n2), lambda b, g: (0, 0))],
        out_specs=pl.BlockSpec((seq, c), lambda b, g: (b, g)),
        out_shape=jax.ShapeDtypeStruct((batch * seq, groups * c), F32),
        scratch_shapes=[pltpu.VMEM((seq, c), F32), pltpu.VMEM((seq, c), F32), pltpu.VMEM((2 * seq, c), F32)],
        compiler_params=_cparams(("parallel", "parallel")), name="fourier_long",
    )(f, jnp.asarray(wch, BF16), jnp.asarray(ma, BF16), jnp.asarray(mb, BF16))


def _fourier_short_body(f_ref, wch_ref, mp_ref, o_ref):
    c = FNET_GROUP_DIM
    p = jnp.dot(f_ref[...].astype(BF16), wch_ref[...], preferred_element_type=F32)
    z = jnp.concatenate([p[:, :c], p[:, c:]], axis=0).astype(BF16)
    o_ref[...] = jnp.dot(mp_ref[...], z, preferred_element_type=F32).astype(o_ref.dtype)


def fourier_short(f, *, batch, n, row_block0):
    c = FNET_GROUP_DIM
    groups = f.shape[1] // c
    norm = 1.0 / math.sqrt(n * c)
    cc, sc = _dft_cs(c)
    wch = np.concatenate([cc, -sc], axis=1)
    cn, sn = _dft_cs(n)
    mp = np.concatenate([cn, sn], axis=1) * norm
    return pl.pallas_call(
        _fourier_short_body, grid=(batch, groups),
        in_specs=[pl.BlockSpec((n, c), lambda b, g: (row_block0 + b, g)),
                  pl.BlockSpec((c, 2 * c), lambda b, g: (0, 0)),
                  pl.BlockSpec((n, 2 * n), lambda b, g: (0, 0))],
        out_specs=pl.BlockSpec((n, c), lambda b, g: (b, g)),
        out_shape=jax.ShapeDtypeStruct((batch * n, groups * c), F32),
        compiler_params=_cparams(("parallel", "parallel")), name="fourier_short",
    )(f, jnp.asarray(wch, BF16), jnp.asarray(mp, BF16))


LOG2E = 1.4426950408889634


ONES_ROWS = 16


def _diff_body(lam_ref, q_ref, kl_ref, kn_ref, vtl_ref, kc_ref, vtc_ref, g_ref, o_ref,
               qs_ref, m_ref, acc_ref, st0_ref, *, nk, out_scale):
    j = pl.program_id(3)
    d = DIFF_HEAD_DIM

    def scores(k, comp):
        return lax.dot_general(k, qs_ref[comp], (((1,), (1,)), ((), ())),
                               preferred_element_type=F32)

    @pl.when(j == 0)
    def _():
        q = q_ref[...].astype(F32) * (d ** -0.5 * LOG2E)
        lane = lax.broadcasted_iota(jnp.int32, q.shape, 1)
        qs_ref[0] = jnp.where(lane < d, q, 0.0).astype(BF16)
        qs_ref[1] = jnp.where(lane >= d, q, 0.0).astype(BF16)
        m_ref[...] = jnp.full(m_ref.shape, NEG, F32)
        acc_ref[...] = jnp.zeros(acc_ref.shape, F32)
        st0_ref[...] = scores(kl_ref[...], 0)

    def softmax_pv(comp, st, vt):
        m_old = m_ref[comp]
        m_new = jnp.maximum(m_old, jnp.max(st, axis=0, keepdims=True))
        alpha = jnp.exp2(m_old - m_new)
        p = jnp.exp2((st - m_new).astype(BF16))
        acc_ref[comp] = alpha * acc_ref[comp] + jnp.dot(vt, p, preferred_element_type=F32)
        m_ref[comp] = m_new

    def step(k, vt, k_next):
        st1 = scores(k, 1)
        softmax_pv(0, st0_ref[:k.shape[0], :], vt)
        if k_next is not None:
            st0_ref[:k_next.shape[0], :] = scores(k_next, 0)
        softmax_pv(1, st1, vt)

    @pl.when(j < nk - 1)
    def _():
        step(kl_ref[...], vtl_ref[...], kn_ref[...])

    @pl.when(j == nk - 1)
    def _():
        step(kl_ref[...], vtl_ref[...], kc_ref[...])

    @pl.when(j == nk)
    def _():
        step(kc_ref[...], vtc_ref[...], None)
        lam = lam_ref[0]
        n = 2 * d
        o0 = acc_ref[0, :n, :] / acc_ref[0, n:n + 1, :]
        o1 = acc_ref[1, :n, :] / acc_ref[1, n:n + 1, :]
        ot = o0 - lam * o1
        yt = ot * lax.rsqrt(jnp.mean(ot * ot, axis=0, keepdims=True) + EPS)
        o_ref[...] = (yt.T * g_ref[...] * out_scale).astype(o_ref.dtype)


def diff_attention(qkv, vt, lam, subln_g, lambda_init, *, batch, seq, ctx_len, tq=512, tk=1024):
    h = DIFF_HEADS
    nq = seq // tq
    nk = seq // tk
    vrows = 2 * DIFF_HEAD_DIM + ONES_ROWS
    ctx_blk0 = (batch * seq) // ctx_len
    body = functools.partial(_diff_body, nk=nk, out_scale=1.0 - lambda_init)
    assert nk >= 2 and ctx_len <= tk
    lat = lambda b, hh, i, j: b * nk + jnp.minimum(j, nk - 1)
    nxt = lambda b, hh, i, j: b * nk + jnp.minimum(j + 1, nk - 1)
    return pl.pallas_call(
        body, grid=(batch, h, nq, nk + 1),
        in_specs=[pl.BlockSpec(memory_space=pltpu.SMEM),
                  pl.BlockSpec((tq, LANES), lambda b, hh, i, j: (b * nq + i, hh)),
                  pl.BlockSpec((tk, LANES), lambda b, hh, i, j: (lat(b, hh, i, j), h + hh)),
                  pl.BlockSpec((tk, LANES), lambda b, hh, i, j: (nxt(b, hh, i, j), h + hh)),
                  pl.BlockSpec((vrows, tk), lambda b, hh, i, j: (hh, lat(b, hh, i, j))),
                  pl.BlockSpec((ctx_len, LANES), lambda b, hh, i, j: (ctx_blk0 + b, h + hh)),
                  pl.BlockSpec((vrows, ctx_len), lambda b, hh, i, j: (hh, ctx_blk0 + b)),
                  pl.BlockSpec((1, LANES), lambda b, hh, i, j: (0, 0))],
        out_specs=pl.BlockSpec((tq, LANES), lambda b, hh, i, j: (b * nq + i, hh)),
        out_shape=jax.ShapeDtypeStruct((batch * seq, h * LANES), BF16),
        scratch_shapes=[pltpu.VMEM((2, tq, LANES), BF16), pltpu.VMEM((2, 1, tq), F32),
                        pltpu.VMEM((2, vrows, tq), F32), pltpu.VMEM((tk, tq), F32)],
        compiler_params=_cparams(("parallel", "parallel", "parallel", "arbitrary")),
        name="diff_attention",
    )(lam, qkv, qkv, qkv, vt, qkv, vt, subln_g)


def _with_ones_rows(vt, heads):
    hw, rows = vt.shape
    blocks = vt.reshape(heads, hw // heads, rows)
    ones = jnp.ones((heads, ONES_ROWS, rows), vt.dtype)
    return jnp.concatenate([blocks, ones], axis=1).reshape(hw + heads * ONES_ROWS, rows)


def _extract16(s, pos):
    big = float(2 ** 20)
    rank = jnp.full(s.shape, float(PEER_TOPK), F32)
    vals = []
    for r in range(PEER_TOPK):
        m = jnp.max(s, axis=0, keepdims=True)
        sel = jnp.min(jnp.where(s == m, pos, big), axis=0, keepdims=True)
        hit = pos == sel
        vals.append(m)
        rank = jnp.where(hit, float(r), rank)
        s = jnp.where(hit, -jnp.inf, s)
    return vals, rank, rank < float(PEER_TOPK)


_CAND_GROUPS = ((0, 16), (1, 8), (2, 5), (3, 4), (4, 3), (5, 2), (6, 2), (7, 2))


def _peer_topk_body(q_ref, keys_ref, r1_ref, e1_ref, n0_ref, c0_ref):
    q = q_ref[...]
    kd = q.shape[1] // 2
    tb = q.shape[0]
    res = []
    for half in range(2):
        s = lax.dot_general(keys_ref[0, half], q[:, half * kd:(half + 1) * kd],
                            (((1,), (1,)), ((), ())), preferred_element_type=F32)
        pos = lax.broadcasted_iota(jnp.int32, s.shape, 0).astype(F32)
        vals, rank, _ = _extract16(s, pos)
        res.append((s, vals, rank))
    (s0, v0, rank0), (s1, v1, rank1) = res
    v1_lo = jnp.concatenate(v1[:SUBLANES], axis=0)
    v1_hi = jnp.concatenate(v1[SUBLANES:], axis=0)
    row8 = lax.broadcasted_iota(jnp.int32, (SUBLANES, tb), 0)
    pieces, poses = [], []
    for a, nb in _CAND_GROUPS:
        if nb > SUBLANES:
            pieces += [v0[a] + v1_lo, v0[a] + v1_hi]
            poses += [a * PEER_TOPK + row8, a * PEER_TOPK + SUBLANES + row8]
        else:
            pieces.append(jnp.where(row8 < nb, v0[a] + v1_lo, -jnp.inf))
            poses.append(a * PEER_TOPK + row8)
    pieces.append(jnp.concatenate(v0[SUBLANES:], axis=0) + v1[0])
    poses.append((row8 + SUBLANES) * PEER_TOPK)
    cand = jnp.concatenate(pieces, axis=0)
    cpos = jnp.concatenate(poses, axis=0).astype(F32)
    best, _, taken = _extract16(cand, cpos)
    z = sum(jnp.exp(b - best[0]) for b in best)
    takenf = taken.astype(F32)
    counts = []
    row = 0
    for a, nb in _CAND_GROUPS:
        rows = 2 * SUBLANES if nb > SUBLANES else SUBLANES
        counts.append(jnp.sum(takenf[row:row + rows], axis=0, keepdims=True))
        row += rows
    counts += [takenf[row + i:row + i + 1] for i in range(SUBLANES)]
    n0 = jnp.zeros(s0.shape, F32)
    for a in range(PEER_TOPK):
        n0 = jnp.where(rank0 == float(a), counts[a], n0)
    r1_ref[0] = rank1.astype(r1_ref.dtype)
    e1_ref[0] = jnp.exp(s1 - v1[0]).astype(e1_ref.dtype)
    n0_ref[0] = n0
    c0_ref[0] = jnp.exp(s0 - v0[0]) / z


def peer_topk(q, sub_keys, *, tb=512):
    t = q.shape[0]
    h = PEER_HEADS
    spec = pl.BlockSpec((1, PEER_N_KEYS, tb), lambda i, hh: (hh, 0, i))
    shapes = [jax.ShapeDtypeStruct((h, PEER_N_KEYS, t), dt) for dt in (BF16, BF16, F32, F32)]
    return pl.pallas_call(
        _peer_topk_body, grid=(t // tb, h),
        in_specs=[pl.BlockSpec((tb, 2 * LANES), lambda i, hh: (i, hh)),
                  pl.BlockSpec((1, 2, PEER_N_KEYS, LANES), lambda i, hh: (hh, 0, 0, 0))],
        out_specs=[spec] * 4, out_shape=shapes,
        compiler_params=_cparams(("parallel", "parallel")), name="peer_topk",
    )(q, sub_keys)


PEER_PIPE_LAG = 2
GELU_C1 = math.sqrt(2.0 / math.pi)
GELU_C2 = 0.044715 * GELU_C1


def _peer_mlp_body(ht_ref, u_ref, vt_ref, r1_ref, e1_ref, n0_ref, c0_ref, x_ref, gate_ref, o_ref,
                   at_ref, wt_ref, acc_ref, *, blocks_per_step, n_chunks, n_flat):
    s = pl.program_id(0)
    cur = s % 2
    prev = 1 - cur
    c1 = jnp.clip(s - 1, 0, n_flat - 1) % n_chunks
    c2 = jnp.clip(s - 2, 0, n_flat - 1) % n_chunks

    @pl.when(s == 0)
    def _():
        at_ref[...] = jnp.zeros(at_ref.shape, F32)
        wt_ref[...] = jnp.zeros(wt_ref.shape, BF16)

    @pl.when(c2 == 0)
    def _():
        acc_ref[...] = jnp.zeros(acc_ref.shape, F32)

    acc_ref[...] += jnp.dot(vt_ref[...], wt_ref[prev], preferred_element_type=F32)

    tiles = []
    for ib in range(blocks_per_step):
        i = c1 * blocks_per_step + ib
        g = None
        for h in range(PEER_HEADS):
            n0 = n0_ref[h, pl.ds(i, 1), :].astype(BF16)
            c0 = c0_ref[h, pl.ds(i, 1), :].astype(BF16)
            term = jnp.where(r1_ref[h] < n0, e1_ref[h], jnp.zeros((), BF16)) * c0
            g = term if g is None else g + term
        tiles.append(g)
    gt = jnp.concatenate(tiles, axis=0)
    a = at_ref[prev].astype(BF16)
    inner = a * (GELU_C1 + GELU_C2 * (a * a))
    half = 0.5 * a
    wt_ref[cur] = (half + half * jnp.tanh(inner)) * gt

    at_ref[cur] = jnp.dot(u_ref[...], ht_ref[...], preferred_element_type=F32)

    @pl.when((c2 == n_chunks - 1) & (s >= PEER_PIPE_LAG))
    def _():
        o_ref[...] = x_ref[...] + gate_ref[0] * acc_ref[...].T


def peer_experts(x, ht, routing, u, vt, g_f, *, tm=512, te=1024, seg_rows=8192):
    t, d = x.shape
    n_exp = u.shape[0]
    n_seg = g_f.shape[0]
    n_chunks = n_exp // te
    n_flat = (t // tm) * n_chunks
    seg = functools.partial(_seg_of_block, tm=tm, seg_rows=seg_rows, n_seg=n_seg)

    def lagged(lag):
        f = lambda s: jnp.clip(s - lag, 0, n_flat - 1)
        return (lambda s: f(s) // n_chunks), (lambda s: f(s) % n_chunks)

    (blk0, chunk0), (blk1, _), (blk2, chunk2) = lagged(0), lagged(1), lagged(2)
    route_spec = pl.BlockSpec((PEER_HEADS, PEER_N_KEYS, tm), lambda s: (0, 0, blk1(s)))
    body = functools.partial(_peer_mlp_body, blocks_per_step=te // PEER_N_KEYS, n_chunks=n_chunks, n_flat=n_flat)
    return pl.pallas_call(
        body, grid=(n_flat + PEER_PIPE_LAG,),
        in_specs=[pl.BlockSpec((d, tm), lambda s: (0, blk0(s))),
                  pl.BlockSpec((te, d), lambda s: (chunk0(s), 0)),
                  pl.BlockSpec((d, te), lambda s: (0, chunk2(s))),
                  route_spec, route_spec, route_spec, route_spec,
                  pl.BlockSpec((tm, d), lambda s: (blk2(s), 0)),
                  pl.BlockSpec((1, 1, d), lambda s: (seg(blk2(s)), 0, 0))],
        out_specs=pl.BlockSpec((tm, d), lambda s: (blk2(s), 0)),
        out_shape=jax.ShapeDtypeStruct((t, d), F32),
        scratch_shapes=[pltpu.VMEM((2, te, tm), F32), pltpu.VMEM((2, te, tm), BF16), pltpu.VMEM((d, tm), F32)],
        compiler_params=_cparams(("arbitrary",)), name="peer_experts",
    )(ht, u, vt, *routing, x, g_f)


def _rms_body(x_ref, g_ref, o_ref):
    xf = x_ref[...]
    o_ref[...] = xf * lax.rsqrt(jnp.mean(xf * xf, axis=-1, keepdims=True) + EPS) * g_ref[...]


def final_rms_norm(x, g, *, tm=1024):
    m, d = x.shape
    return pl.pallas_call(
        _rms_body, grid=(m // tm,),
        in_specs=[pl.BlockSpec((tm, d), lambda i: (i, 0)), pl.BlockSpec((1, d), lambda i: (0, 0))],
        out_specs=pl.BlockSpec((tm, d), lambda i: (i, 0)),
        out_shape=jax.ShapeDtypeStruct((m, d), F32),
        compiler_params=_cparams(("parallel",)), name="final_rms_norm")(x, g)


def _rope_tables(seq, extra_rows):
    quarter = DIFF_HEAD_DIM // 4
    t = jnp.arange(seq)
    row = (t // GRID_W).astype(F32)
    col = (t % GRID_W).astype(F32)
    inv = ROPE_BASE ** (-jnp.arange(quarter, dtype=F32) / quarter)
    cos_r, sin_r = jnp.cos(row[:, None] * inv), jnp.sin(row[:, None] * inv)
    cos_c, sin_c = jnp.cos(col[:, None] * inv), jnp.sin(col[:, None] * inv)
    cos64 = jnp.concatenate([cos_r, cos_r, cos_c, cos_c], axis=1)
    sin64 = jnp.concatenate([-sin_r, sin_r, -sin_c, sin_c], axis=1)
    cos = jnp.concatenate([cos64, cos64], axis=1)
    sin = jnp.concatenate([sin64, sin64], axis=1)
    cos = jnp.concatenate([cos, jnp.ones((extra_rows, LANES), F32)], axis=0)
    sin = jnp.concatenate([sin, jnp.zeros((extra_rows, LANES), F32)], axis=0)
    return cos, sin


def kernel(x, c, ctx, c_ctx, w_mod, b_mod, norm_mix_g, norm_ffn_g, even_w_in, even_w_out, na_rpb,
           odd_w_in, odd_w_out, diff_lambda_q1, diff_lambda_k1, diff_lambda_q2, diff_lambda_k2,
           diff_subln_g, peer_w_q, peer_sub_keys, peer_u, peer_v, final_norm_g):
    batch, seq, d = x.shape
    ctx_len = ctx.shape[1]
    depth = w_mod.shape[0]
    n_lat = batch * seq
    tm = 512
    xs = jnp.concatenate([x.reshape(n_lat, d), ctx.reshape(batch * ctx_len, d)], axis=0)
    cond = jnp.concatenate([c, c_ctx[None], jnp.zeros((SUBLANES - batch - 1, d), F32)], axis=0)
    cond = jax.nn.silu(cond)
    n_seg = batch + 1

    for layer in range(depth):
        last = layer == depth - 1
        jl = layer // 2
        mod = fused_matmul(cond, w_mod[layer].astype(BF16), tm=SUBLANES, tn=1024)[:n_seg] + b_mod[layer]
        sh_m, sc_m, g_m, sh_f, sc_f, g_f = [m_[:, None, :] for m_ in jnp.split(mod, 6, axis=-1)]
        norm_mix = (norm_mix_g[layer][None], sh_m, sc_m)
        norm_ffn = (norm_ffn_g[layer][None], sh_f, sc_f)
        if layer % 2 == 0:
            w_in = even_w_in[jl].astype(BF16)
            w_out = even_w_out[jl].astype(BF16)
            qkv = fused_matmul(xs, w_in[:, :3 * NA_WIDTH], norm=norm_mix, out_dtype=BF16, tm=tm)
            f = fused_matmul(xs, w_in[:, 3 * NA_WIDTH:], norm=norm_mix, tm=tm)
            attn = neighbourhood_attention(qkv, na_rpb[jl], batch=batch, seq=seq, ctx_len=ctx_len)
            four = fourier_long(f, batch=batch, seq=seq)
            if not last:
                attn = jnp.concatenate([attn, context_attention(qkv, batch=batch, seq=seq, ctx_len=ctx_len)], axis=0)
                four = jnp.concatenate([four, fourier_short(f, batch=batch, n=ctx_len, row_block0=n_lat // ctx_len)], axis=0)
            rows = attn.shape[0]
            xs_mix = fused_matmul(attn, w_out[:NA_WIDTH], res=(xs[:rows], g_m), tm=tm)
            xs_mix = fused_matmul(four, w_out[NA_WIDTH:], res=(xs_mix, g_m), tm=tm)
        else:
            lambda_init = 0.8 - 0.6 * math.exp(-0.3 * layer)
            lam = (jnp.exp(jnp.sum(diff_lambda_q1[jl] * diff_lambda_k1[jl]))
                   - jnp.exp(jnp.sum(diff_lambda_q2[jl] * diff_lambda_k2[jl])) + lambda_init).reshape(1)
            cos, sin = _rope_tables(seq, tm)
            v_width = DIFF_HEADS * 2 * DIFF_HEAD_DIM
            qkv, vt = fused_matmul(xs, odd_w_in[jl].astype(BF16), norm=norm_mix, out_dtype=BF16, tm=tm,
                                   rope=(cos, sin, 2 * DIFF_HEADS * 2 * DIFF_HEAD_DIM, seq // tm, n_lat // tm),
                                   t_cols=v_width)
            o = diff_attention(qkv, _with_ones_rows(vt, DIFF_HEADS), lam, diff_subln_g[jl][None], lambda_init,
                               batch=batch, seq=seq, ctx_len=ctx_len)
            if not last:
                raise NotImplementedError("context output of a differential-attention layer")
            xs_mix = fused_matmul(o, odd_w_out[jl].astype(BF16), res=(xs[:n_lat], g_m), tm=tm)
        q, ht = fused_matmul(xs_mix, peer_w_q[layer].astype(BF16), norm=norm_ffn, emit_ht=True, tm=tm)
        routing = peer_topk(q, peer_sub_keys[layer])
        xs = peer_experts(xs_mix, ht, routing, peer_u[layer].astype(BF16), peer_v[layer].astype(BF16).T, g_f)
        if not last and xs.shape[0] == n_lat:
            raise NotImplementedError("context stream dropped before the last layer")
    out = final_rms_norm(xs[:n_lat], final_norm_g[None])
    return out.reshape(batch, seq, d)
```

```python
import functools
import math

import jax
import jax.numpy as jnp
import numpy as np
from jax import lax
from jax.experimental import pallas as pl
from jax.experimental.pallas import tpu as pltpu

F32 = jnp.float32
BF16 = jnp.bfloat16

D_MODEL = 1024
GRID_W = 64
EPS = 1e-6
ROPE_BASE = 10000.0
NA_HEADS = 8
NA_HEAD_DIM = 64
NA_WIDTH = 512
NA_WIN_H = 8
NA_WIN_W = 16
FNET_GROUP_DIM = 128
DIFF_HEADS = 8
DIFF_HEAD_DIM = 64
PEER_HEADS = 8
PEER_N_KEYS = 128
PEER_TOPK = 16
LANES = 128
SUBLANES = 8
NEG = -1e30
VMEM_LIMIT = 56 * 1024 * 1024


def _cparams(sem, vmem=VMEM_LIMIT):
    return pltpu.CompilerParams(dimension_semantics=sem, vmem_limit_bytes=vmem)


def _mm_body(*refs, has_norm, emit_ht, rope_tiles, has_res, t_from):
    it = iter(refs)
    x_ref = next(it)
    if has_norm:
        g_ref, sh_ref, sc_ref = next(it), next(it), next(it)
    w_ref = next(it)
    if rope_tiles:
        cos_ref, sin_ref = next(it), next(it)
    if has_res:
        res_ref, gate_ref = next(it), next(it)
    o_ref = next(it)
    if emit_ht:
        ht_ref = next(it)
    if t_from is not None:
        ot_ref = next(it)
    if has_norm:
        xs_ref = next(it)
    j = pl.program_id(1)

    if has_norm:
        @pl.when(j == 0)
        def _():
            xf = x_ref[...]
            y = xf * lax.rsqrt(jnp.mean(xf * xf, axis=-1, keepdims=True) + EPS)
            y = y * g_ref[...] * (1.0 + sc_ref[0]) + sh_ref[0]
            xs_ref[...] = y.astype(BF16)
            if emit_ht:
                ht_ref[...] = y.T.astype(BF16)
        a = xs_ref[...]
    else:
        a = x_ref[...].astype(BF16)
    acc = jnp.dot(a, w_ref[...], preferred_element_type=F32)

    if t_from is not None:
        @pl.when(j >= t_from)
        def _():
            ot_ref[...] = acc.T.astype(ot_ref.dtype)

    def finish(v):
        if has_res:
            v = res_ref[...] + gate_ref[0] * v
        o_ref[...] = v.astype(o_ref.dtype)

    if rope_tiles:
        @pl.when(j < rope_tiles)
        def _():
            tn = acc.shape[1]
            reps = tn // LANES
            c = jnp.concatenate([cos_ref[...]] * reps, axis=1)
            s = jnp.concatenate([sin_ref[...]] * reps, axis=1)
            lane = lax.broadcasted_iota(jnp.int32, acc.shape, 1)
            first = (lane % 32) < 16
            partner = jnp.where(first, pltpu.roll(acc, tn - 16, 1), pltpu.roll(acc, 16, 1))
            finish(acc * c + partner * s)

        @pl.when(j >= rope_tiles)
        def _():
            finish(acc)
    else:
        finish(acc)


def _seg_of_block(i, tm, seg_rows, n_seg):
    return jnp.minimum((i * tm) // seg_rows, n_seg - 1)


def fused_matmul(x, w, *, norm=None, rope=None, res=None, emit_ht=False, t_cols=0, out_dtype=F32,
                 tm=512, tn=1024, seg_rows=8192):
    m, k = x.shape
    n = w.shape[1]
    tm = min(tm, m)
    tn = max(t for t in range(LANES, min(tn, n) + 1, LANES) if n % t == 0)
    assert m % tm == 0 and n % tn == 0
    grid = (m // tm, n // tn)
    in_specs = [pl.BlockSpec((tm, k), lambda i, j: (i, 0))]
    args = [x]
    n_seg = 1
    if norm is not None:
        g, sh, sc = norm
        n_seg = sh.shape[0]
        seg = functools.partial(_seg_of_block, tm=tm, seg_rows=seg_rows, n_seg=n_seg)
        in_specs += [pl.BlockSpec((1, k), lambda i, j: (0, 0)),
                     pl.BlockSpec((1, 1, k), lambda i, j: (seg(i), 0, 0)),
                     pl.BlockSpec((1, 1, k), lambda i, j: (seg(i), 0, 0))]
        args += [g, sh, sc]
    in_specs.append(pl.BlockSpec((k, tn), lambda i, j: (0, j)))
    args.append(w)
    rope_tiles = 0
    if rope is not None:
        cos, sin, n_rope_cols, n_pos_blocks, n_lat_blocks = rope
        rope_tiles = n_rope_cols // tn
        pos = lambda i, j: (jnp.where(i < n_lat_blocks, i % n_pos_blocks, n_pos_blocks), 0)
        in_specs += [pl.BlockSpec((tm, LANES), pos), pl.BlockSpec((tm, LANES), pos)]
        args += [cos, sin]
    if res is not None:
        r, gate = res
        n_seg_r = gate.shape[0]
        segr = functools.partial(_seg_of_block, tm=tm, seg_rows=seg_rows, n_seg=n_seg_r)
        in_specs += [pl.BlockSpec((tm, tn), lambda i, j: (i, j)),
                     pl.BlockSpec((1, 1, tn), lambda i, j: (segr(i), 0, j))]
        args += [r, gate]
    out_shape = [jax.ShapeDtypeStruct((m, n), out_dtype)]
    out_specs = [pl.BlockSpec((tm, tn), lambda i, j: (i, j))]
    if emit_ht:
        out_shape.append(jax.ShapeDtypeStruct((k, m), BF16))
        out_specs.append(pl.BlockSpec((k, tm), lambda i, j: (0, i)))
    t_from = None
    if t_cols:
        assert t_cols % tn == 0
        t_from = (n - t_cols) // tn
        out_shape.append(jax.ShapeDtypeStruct((t_cols, m), BF16))
        out_specs.append(pl.BlockSpec((tn, tm), lambda i, j: (jnp.maximum(j - t_from, 0), i)))
    scratch = [pltpu.VMEM((tm, k), BF16)] if norm is not None else []
    body = functools.partial(_mm_body, has_norm=norm is not None, emit_ht=emit_ht,
                             rope_tiles=rope_tiles, has_res=res is not None, t_from=t_from)
    outs = pl.pallas_call(
        body, grid=grid, in_specs=in_specs, out_specs=out_specs, out_shape=out_shape,
        scratch_shapes=scratch, compiler_params=_cparams(("parallel", "arbitrary")),
        name="fused_matmul")(*args)
    return outs if len(outs) > 1 else outs[0]


NA_GROUP = 4
NA_UNION = NA_WIN_H + NA_GROUP


def _na_body(q_ref, k_ref, v_ref, kc_ref, vc_ref, bias_ref, o_ref, *, rows):
    nq = NA_GROUP * GRID_W
    nkeys = NA_UNION * GRID_W
    lane = lax.broadcasted_iota(jnp.int32, (nq, LANES), 1)
    head_masks = [lane < NA_HEAD_DIM, lane >= NA_HEAD_DIM]
    kc = kc_ref[...]
    vc = vc_ref[...]
    scale = NA_HEAD_DIM ** -0.5
    nt = (((1,), (1,)), ((), ()))
    n_groups = rows // NA_GROUP

    def group_body(g, carry):
        r0 = g * NA_GROUP
        u0 = jnp.clip(r0 - NA_WIN_H // 2, 0, rows - NA_UNION)
        pattern = jnp.where(g == 0, 0, jnp.where(g == n_groups - 1, 2, 1))
        q_rows = pl.ds(pl.multiple_of(r0 * GRID_W, nq), nq)
        q = q_ref[q_rows, :]
        k_start = pl.multiple_of(u0 * GRID_W, NA_GROUP * GRID_W)
        kw = k_ref[pl.ds(k_start, nkeys), :]
        vw = v_ref[pl.ds(k_start, nkeys), :]
        outs = []
        for hh in range(2):
            qm = jnp.where(head_masks[hh], q, jnp.zeros_like(q))
            bias = bias_ref[pattern, :, hh].reshape(nq, nkeys)
            s_nb = lax.dot_general(qm, kw, nt, preferred_element_type=F32) * scale + bias
            s_cx = lax.dot_general(qm, kc, nt, preferred_element_type=F32) * scale
            m = jnp.maximum(jnp.max(s_nb, axis=1, keepdims=True), jnp.max(s_cx, axis=1, keepdims=True))
            p_nb = jnp.exp(s_nb - m)
            p_cx = jnp.exp(s_cx - m)
            l = jnp.sum(p_nb, axis=1, keepdims=True) + jnp.sum(p_cx, axis=1, keepdims=True)
            o = (jnp.dot(p_nb.astype(BF16), vw, preferred_element_type=F32)
                 + jnp.dot(p_cx.astype(BF16), vc, preferred_element_type=F32))
            outs.append(o / l)
        o_ref[q_rows, :] = jnp.where(head_masks[0], outs[0], outs[1]).astype(o_ref.dtype)
        return carry

    lax.fori_loop(0, n_groups, group_body, 0)


def _na_bias_table(rpb):
    col = np.arange(GRID_W)
    col_start = np.clip(col - NA_WIN_W // 2, 0, GRID_W - NA_WIN_W)
    kc = np.arange(GRID_W)
    inside = (kc[None, :] >= col_start[:, None]) & (kc[None, :] < col_start[:, None] + NA_WIN_W)
    dc = np.clip(kc[None, :] - col[:, None] + NA_WIN_W - 1, 0, 2 * NA_WIN_W - 2)
    per_dr = jnp.where(inside[None, None], rpb[:, :, dc], NEG)
    masked = jnp.full((NA_HEADS, GRID_W, GRID_W), NEG, F32)
    half = NA_WIN_H // 2
    patterns = [(0, lambda dr: -dr), (-half, lambda dr: -half), (-NA_WIN_H, lambda dr: -half - dr)]
    tabs = []
    for union_off, win_start in patterns:
        per_row = []
        for dr in range(NA_GROUP):
            blocks = []
            for kk in range(NA_UNION):
                delta = union_off + kk - dr
                ok = win_start(dr) <= delta < win_start(dr) + NA_WIN_H
                blocks.append(per_dr[:, delta + NA_WIN_H - 1] if ok else masked)
            per_row.append(jnp.concatenate(blocks, axis=-1))
        tabs.append(jnp.stack(per_row))
    return jnp.stack(tabs)


def neighbourhood_attention(qkv, rpb, *, batch, seq, ctx_len):
    rows = seq // GRID_W
    assert rows % NA_GROUP == 0 and rows >= NA_UNION + NA_GROUP
    bias = _na_bias_table(rpb)
    hp = NA_HEADS // 2
    ctx_blk0 = (batch * seq) // ctx_len
    body = functools.partial(_na_body, rows=rows)
    return pl.pallas_call(
        body, grid=(batch, hp),
        in_specs=[pl.BlockSpec((seq, LANES), lambda b, h: (b, h)),
                  pl.BlockSpec((seq, LANES), lambda b, h: (b, hp + h)),
                  pl.BlockSpec((seq, LANES), lambda b, h: (b, 2 * hp + h)),
                  pl.BlockSpec((ctx_len, LANES), lambda b, h: (ctx_blk0 + b, hp + h)),
                  pl.BlockSpec((ctx_len, LANES), lambda b, h: (ctx_blk0 + b, 2 * hp + h)),
                  pl.BlockSpec((3, NA_GROUP, 2, GRID_W, NA_UNION * GRID_W), lambda b, h: (0, 0, h, 0, 0))],
        out_specs=pl.BlockSpec((seq, LANES), lambda b, h: (b, h)),
        out_shape=jax.ShapeDtypeStruct((batch * seq, NA_WIDTH), BF16),
        compiler_params=_cparams(("parallel", "parallel")), name="neighbourhood_attention",
    )(qkv, qkv, qkv, qkv, qkv, bias)


def _ctx_attn_body(q_ref, k_ref, v_ref, o_ref):
    lane = lax.broadcasted_iota(jnp.int32, q_ref.shape, 1)
    q = q_ref[...]
    k = k_ref[...]
    v = v_ref[...]
    scale = NA_HEAD_DIM ** -0.5
    outs = []
    for hh in range(2):
        msk = (lane < NA_HEAD_DIM) if hh == 0 else (lane >= NA_HEAD_DIM)
        qm = jnp.where(msk, q, jnp.zeros_like(q))
        s = lax.dot_general(qm, k, (((1,), (1,)), ((), ())), preferred_element_type=F32) * scale
        m = jnp.max(s, axis=1, keepdims=True)
        p = jnp.exp(s - m)
        l = jnp.sum(p, axis=1, keepdims=True)
        outs.append(jnp.dot(p.astype(BF16), v, preferred_element_type=F32) / l)
    o_ref[...] = jnp.where(lane < NA_HEAD_DIM, outs[0], outs[1]).astype(o_ref.dtype)


def context_attention(qkv, *, batch, seq, ctx_len):
    hp = NA_HEADS // 2
    ctx_blk0 = (batch * seq) // ctx_len
    return pl.pallas_call(
        _ctx_attn_body, grid=(batch, hp),
        in_specs=[pl.BlockSpec((ctx_len, LANES), lambda b, h: (ctx_blk0 + b, h)),
                  pl.BlockSpec((ctx_len, LANES), lambda b, h: (ctx_blk0 + b, hp + h)),
                  pl.BlockSpec((ctx_len, LANES), lambda b, h: (ctx_blk0 + b, 2 * hp + h))],
        out_specs=pl.BlockSpec((ctx_len, LANES), lambda b, h: (b, h)),
        out_shape=jax.ShapeDtypeStruct((batch * ctx_len, NA_WIDTH), BF16),
        compiler_params=_cparams(("parallel", "parallel")), name="context_attention",
    )(qkv, qkv, qkv)


def _dft_cs(n):
    k = np.arange(n)
    ang = 2.0 * np.pi * ((k[:, None] * k[None, :]) % n) / n
    return np.cos(ang), np.sin(ang)


def _fourier_long_body(f_ref, wch_ref, ma_ref, mb_ref, o_ref, pr_ref, pi_ref, z_ref, *, n1, n2):
    c = FNET_GROUP_DIM
    p = jnp.dot(f_ref[...].astype(BF16), wch_ref[...], preferred_element_type=F32)
    pr_ref[...] = p[:, :c]
    pi_ref[...] = p[:, c:]

    def stage_a(j, carry):
        rows = pl.ds(j, n1, stride=n2)
        x = jnp.concatenate([pr_ref[rows, :], pi_ref[rows, :]], axis=1).astype(BF16)
        y = jnp.dot(ma_ref[j], x, preferred_element_type=F32)
        zr = y[:n1, :c] - y[n1:, c:]
        zi = y[n1:, :c] + y[:n1, c:]
        z_ref[pl.ds(j, n1, stride=2 * n2), :] = zr
        z_ref[pl.ds(n2 + j, n1, stride=2 * n2), :] = zi
        return carry

    lax.fori_loop(0, n2, stage_a, 0)

    def stage_b(k1, carry):
        z = z_ref[pl.ds(pl.multiple_of(k1 * 2 * n2, 2 * n2), 2 * n2), :].astype(BF16)
        y = jnp.dot(mb_ref[...], z, preferred_element_type=F32)
        o_ref[pl.ds(k1, n2, stride=n1), :] = y.astype(o_ref.dtype)
        return carry

    lax.fori_loop(0, n1, stage_b, 0)


def fourier_long(f, *, batch, seq):
    c = FNET_GROUP_DIM
    groups = f.shape[1] // c
    n2 = 128
    n1 = seq // n2
    norm = 1.0 / math.sqrt(seq * c)
    cc, sc = _dft_cs(c)
    wch = np.concatenate([cc, -sc], axis=1)
    c1, s1 = _dft_cs(n1)
    j = np.arange(n2)
    k1 = np.arange(n1)
    tw = 2.0 * np.pi * (j[:, None] * k1[None, :]) / seq
    ar = np.cos(tw)[:, :, None] * c1[None] - np.sin(tw)[:, :, None] * s1[None]
    ai = -(np.cos(tw)[:, :, None] * s1[None] + np.sin(tw)[:, :, None] * c1[None])
    ma = np.concatenate([ar, ai], axis=1)
    c2, s2 = _dft_cs(n2)
    mb = np.concatenate([c2, s2], axis=1) * norm
    body = functools.partial(_fourier_long_body, n1=n1, n2=n2)
    return pl.pallas_call(
        body, grid=(batch, groups),
        in_specs=[pl.BlockSpec((seq, c), lambda b, g: (b, g)),
                  pl.BlockSpec((c, 2 * c), lambda b, g: (0, 0)),
                  pl.BlockSpec((n2, 2 * n1, n1), lambda b, g: (0, 0, 0)),
                  pl.BlockSpec((n2, 2 * n2), lambda b, g: (0, 0))],
        out_specs=pl.BlockSpec((seq, c), lambda b, g: (b, g)),
        out_shape=jax.ShapeDtypeStruct((batch * seq, groups * c), F32),
        scratch_shapes=[pltpu.VMEM((seq, c), F32), pltpu.VMEM((seq, c), F32), pltpu.VMEM((2 * seq, c), F32)],
        compiler_params=_cparams(("parallel", "parallel")), name="fourier_long",
    )(f, jnp.asarray(wch, BF16), jnp.asarray(ma, BF16), jnp.asarray(mb, BF16))


def _fourier_short_body(f_ref, wch_ref, mp_ref, o_ref):
    c = FNET_GROUP_DIM
    p = jnp.dot(f_ref[...].astype(BF16), wch_ref[...], preferred_element_type=F32)
    z = jnp.concatenate([p[:, :c], p[:, c:]], axis=0).astype(BF16)
    o_ref[...] = jnp.dot(mp_ref[...], z, preferred_element_type=F32).astype(o_ref.dtype)


def fourier_short(f, *, batch, n, row_block0):
    c = FNET_GROUP_DIM
    groups = f.shape[1] // c
    norm = 1.0 / math.sqrt(n * c)
    cc, sc = _dft_cs(c)
    wch = np.concatenate([cc, -sc], axis=1)
    cn, sn = _dft_cs(n)
    mp = np.concatenate([cn, sn], axis=1) * norm
    return pl.pallas_call(
        _fourier_short_body, grid=(batch, groups),
        in_specs=[pl.BlockSpec((n, c), lambda b, g: (row_block0 + b, g)),
                  pl.BlockSpec((c, 2 * c), lambda b, g: (0, 0)),
                  pl.BlockSpec((n, 2 * n), lambda b, g: (0, 0))],
        out_specs=pl.BlockSpec((n, c), lambda b, g: (b, g)),
        out_shape=jax.ShapeDtypeStruct((batch * n, groups * c), F32),
        compiler_params=_cparams(("parallel", "parallel")), name="fourier_short",
    )(f, jnp.asarray(wch, BF16), jnp.asarray(mp, BF16))


LOG2E = 1.4426950408889634


ONES_ROWS = 16


def _diff_body(lam_ref, q_ref, kl_ref, kn_ref, vtl_ref, kc_ref, vtc_ref, g_ref, o_ref,
               qs_ref, m_ref, acc_ref, st0_ref, *, nk, out_scale):
    j = pl.program_id(3)
    d = DIFF_HEAD_DIM

    def scores(k, comp):
        return lax.dot_general(k, qs_ref[comp], (((1,), (1,)), ((), ())),
                               preferred_element_type=F32)

    @pl.when(j == 0)
    def _():
        q = q_ref[...].astype(F32) * (d ** -0.5 * LOG2E)
        lane = lax.broadcasted_iota(jnp.int32, q.shape, 1)
        qs_ref[0] = jnp.where(lane < d, q, 0.0).astype(BF16)
        qs_ref[1] = jnp.where(lane >= d, q, 0.0).astype(BF16)
        m_ref[...] = jnp.full(m_ref.shape, NEG, F32)
        acc_ref[...] = jnp.zeros(acc_ref.shape, F32)
        st0_ref[...] = scores(kl_ref[...], 0)

    def softmax_pv(comp, st, vt):
        m_old = m_ref[comp]
        m_new = jnp.maximum(m_old, jnp.max(st, axis=0, keepdims=True))
        alpha = jnp.exp2(m_old - m_new)
        p = jnp.exp2((st - m_new).astype(BF16))
        acc_ref[comp] = alpha * acc_ref[comp] + jnp.dot(vt, p, preferred_element_type=F32)
        m_ref[comp] = m_new

    def step(k, vt, k_next):
        st1 = scores(k, 1)
        softmax_pv(0, st0_ref[:k.shape[0], :], vt)
        if k_next is not None:
            st0_ref[:k_next.shape[0], :] = scores(k_next, 0)
        softmax_pv(1, st1, vt)

    @pl.when(j < nk - 1)
    def _():
        step(kl_ref[...], vtl_ref[...], kn_ref[...])

    @pl.when(j == nk - 1)
    def _():
        step(kl_ref[...], vtl_ref[...], kc_ref[...])

    @pl.when(j == nk)
    def _():
        step(kc_ref[...], vtc_ref[...], None)
        lam = lam_ref[0]
        n = 2 * d
        o0 = acc_ref[0, :n, :] / acc_ref[0, n:n + 1, :]
        o1 = acc_ref[1, :n, :] / acc_ref[1, n:n + 1, :]
        ot = o0 - lam * o1
        yt = ot * lax.rsqrt(jnp.mean(ot * ot, axis=0, keepdims=True) + EPS)
        o_ref[...] = (yt.T * g_ref[...] * out_scale).astype(o_ref.dtype)


def diff_attention(qkv, vt, lam, subln_g, lambda_init, *, batch, seq, ctx_len, tq=1024, tk=2048):
    h = DIFF_HEADS
    nq = seq // tq
    nk = seq // tk
    vrows = 2 * DIFF_HEAD_DIM + ONES_ROWS
    ctx_blk0 = (batch * seq) // ctx_len
    body = functools.partial(_diff_body, nk=nk, out_scale=1.0 - lambda_init)
    assert nk >= 2 and ctx_len <= tk
    lat = lambda b, hh, i, j: b * nk + jnp.minimum(j, nk - 1)
    nxt = lambda b, hh, i, j: b * nk + jnp.minimum(j + 1, nk - 1)
    return pl.pallas_call(
        body, grid=(batch, h, nq, nk + 1),
        in_specs=[pl.BlockSpec(memory_space=pltpu.SMEM),
                  pl.BlockSpec((tq, LANES), lambda b, hh, i, j: (b * nq + i, hh)),
                  pl.BlockSpec((tk, LANES), lambda b, hh, i, j: (lat(b, hh, i, j), h + hh)),
                  pl.BlockSpec((tk, LANES), lambda b, hh, i, j: (nxt(b, hh, i, j), h + hh)),
                  pl.BlockSpec((vrows, tk), lambda b, hh, i, j: (hh, lat(b, hh, i, j))),
                  pl.BlockSpec((ctx_len, LANES), lambda b, hh, i, j: (ctx_blk0 + b, h + hh)),
                  pl.BlockSpec((vrows, ctx_len), lambda b, hh, i, j: (hh, ctx_blk0 + b)),
                  pl.BlockSpec((1, LANES), lambda b, hh, i, j: (0, 0))],
        out_specs=pl.BlockSpec((tq, LANES), lambda b, hh, i, j: (b * nq + i, hh)),
        out_shape=jax.ShapeDtypeStruct((batch * seq, h * LANES), BF16),
        scratch_shapes=[pltpu.VMEM((2, tq, LANES), BF16), pltpu.VMEM((2, 1, tq), F32),
                        pltpu.VMEM((2, vrows, tq), F32), pltpu.VMEM((tk, tq), F32)],
        compiler_params=_cparams(("parallel", "parallel", "parallel", "arbitrary")),
        name="diff_attention",
    )(lam, qkv, qkv, qkv, vt, qkv, vt, subln_g)


def _with_ones_rows(vt, heads):
    hw, rows = vt.shape
    blocks = vt.reshape(heads, hw // heads, rows)
    ones = jnp.ones((heads, ONES_ROWS, rows), vt.dtype)
    return jnp.concatenate([blocks, ones], axis=1).reshape(hw + heads * ONES_ROWS, rows)


def _extract16(s, pos):
    big = float(2 ** 20)
    rank = jnp.full(s.shape, float(PEER_TOPK), F32)
    vals = []
    for r in range(PEER_TOPK):
        m = jnp.max(s, axis=0, keepdims=True)
        sel = jnp.min(jnp.where(s == m, pos, big), axis=0, keepdims=True)
        hit = pos == sel
        vals.append(m)
        rank = jnp.where(hit, float(r), rank)
        s = jnp.where(hit, -jnp.inf, s)
    return vals, rank, rank < float(PEER_TOPK)


_CAND_GROUPS = ((0, 16), (1, 8), (2, 5), (3, 4), (4, 3), (5, 2), (6, 2), (7, 2))


def _peer_topk_body(q_ref, keys_ref, r1_ref, e1_ref, n0_ref, c0_ref):
    q = q_ref[...]
    kd = q.shape[1] // 2
    tb = q.shape[0]
    res = []
    for half in range(2):
        s = lax.dot_general(keys_ref[0, half], q[:, half * kd:(half + 1) * kd],
                            (((1,), (1,)), ((), ())), preferred_element_type=F32)
        pos = lax.broadcasted_iota(jnp.int32, s.shape, 0).astype(F32)
        vals, rank, _ = _extract16(s, pos)
        res.append((s, vals, rank))
    (s0, v0, rank0), (s1, v1, rank1) = res
    v1_lo = jnp.concatenate(v1[:SUBLANES], axis=0)
    v1_hi = jnp.concatenate(v1[SUBLANES:], axis=0)
    row8 = lax.broadcasted_iota(jnp.int32, (SUBLANES, tb), 0)
    pieces, poses = [], []
    for a, nb in _CAND_GROUPS:
        if nb > SUBLANES:
            pieces += [v0[a] + v1_lo, v0[a] + v1_hi]
            poses += [a * PEER_TOPK + row8, a * PEER_TOPK + SUBLANES + row8]
        else:
            pieces.append(jnp.where(row8 < nb, v0[a] + v1_lo, -jnp.inf))
            poses.append(a * PEER_TOPK + row8)
    pieces.append(jnp.concatenate(v0[SUBLANES:], axis=0) + v1[0])
    poses.append((row8 + SUBLANES) * PEER_TOPK)
    cand = jnp.concatenate(pieces, axis=0)
    cpos = jnp.concatenate(poses, axis=0).astype(F32)
    best, _, taken = _extract16(cand, cpos)
    z = sum(jnp.exp(b - best[0]) for b in best)
    takenf = taken.astype(F32)
    counts = []
    row = 0
    for a, nb in _CAND_GROUPS:
        rows = 2 * SUBLANES if nb > SUBLANES else SUBLANES
        counts.append(jnp.sum(takenf[row:row + rows], axis=0, keepdims=True))
        row += rows
    counts += [takenf[row + i:row + i + 1] for i in range(SUBLANES)]
    n0 = jnp.zeros(s0.shape, F32)
    for a in range(PEER_TOPK):
        n0 = jnp.where(rank0 == float(a), counts[a], n0)
    r1_ref[0] = rank1.astype(r1_ref.dtype)
    e1_ref[0] = jnp.exp(s1 - v1[0]).astype(e1_ref.dtype)
    n0_ref[0] = n0
    c0_ref[0] = jnp.exp(s0 - v0[0]) / z


def peer_topk(q, sub_keys, *, tb=512):
    t = q.shape[0]
    h = PEER_HEADS
    spec = pl.BlockSpec((1, PEER_N_KEYS, tb), lambda i, hh: (hh, 0, i))
    shapes = [jax.ShapeDtypeStruct((h, PEER_N_KEYS, t), dt) for dt in (BF16, BF16, F32, F32)]
    return pl.pallas_call(
        _peer_topk_body, grid=(t // tb, h),
        in_specs=[pl.BlockSpec((tb, 2 * LANES), lambda i, hh: (i, hh)),
                  pl.BlockSpec((1, 2, PEER_N_KEYS, LANES), lambda i, hh: (hh, 0, 0, 0))],
        out_specs=[spec] * 4, out_shape=shapes,
        compiler_params=_cparams(("parallel", "parallel")), name="peer_topk",
    )(q, sub_keys)


PEER_PIPE_LAG = 2
GELU_C1 = math.sqrt(2.0 / math.pi)
GELU_C2 = 0.044715 * GELU_C1


def _peer_mlp_body(ht_ref, u_ref, vt_ref, r1_ref, e1_ref, n0_ref, c0_ref, x_ref, gate_ref, o_ref,
                   at_ref, wt_ref, acc_ref, *, blocks_per_step, n_chunks, n_flat):
    s = pl.program_id(0)
    cur = s % 2
    prev = 1 - cur
    c1 = jnp.clip(s - 1, 0, n_flat - 1) % n_chunks
    c2 = jnp.clip(s - 2, 0, n_flat - 1) % n_chunks

    @pl.when(s == 0)
    def _():
        at_ref[...] = jnp.zeros(at_ref.shape, F32)
        wt_ref[...] = jnp.zeros(wt_ref.shape, BF16)

    @pl.when(c2 == 0)
    def _():
        acc_ref[...] = jnp.zeros(acc_ref.shape, F32)

    acc_ref[...] += jnp.dot(vt_ref[...], wt_ref[prev], preferred_element_type=F32)

    tiles = []
    for ib in range(blocks_per_step):
        i = c1 * blocks_per_step + ib
        g = None
        for h in range(PEER_HEADS):
            n0 = n0_ref[h, pl.ds(i, 1), :].astype(BF16)
            c0 = c0_ref[h, pl.ds(i, 1), :].astype(BF16)
            term = jnp.where(r1_ref[h] < n0, e1_ref[h], jnp.zeros((), BF16)) * c0
            g = term if g is None else g + term
        tiles.append(g)
    gt = jnp.concatenate(tiles, axis=0)
    a = at_ref[prev].astype(BF16)
    inner = a * (GELU_C1 + GELU_C2 * (a * a))
    half = 0.5 * a
    wt_ref[cur] = (half + half * jnp.tanh(inner)) * gt

    at_ref[cur] = jnp.dot(u_ref[...], ht_ref[...], preferred_element_type=F32)

    @pl.when((c2 == n_chunks - 1) & (s >= PEER_PIPE_LAG))
    def _():
        o_ref[...] = x_ref[...] + gate_ref[0] * acc_ref[...].T


def peer_experts(x, ht, routing, u, vt, g_f, *, tm=512, te=1024, seg_rows=8192):
    t, d = x.shape
    n_exp = u.shape[0]
    n_seg = g_f.shape[0]
    n_chunks = n_exp // te
    n_flat = (t // tm) * n_chunks
    seg = functools.partial(_seg_of_block, tm=tm, seg_rows=seg_rows, n_seg=n_seg)

    def lagged(lag):
        f = lambda s: jnp.clip(s - lag, 0, n_flat - 1)
        return (lambda s: f(s) // n_chunks), (lambda s: f(s) % n_chunks)

    (blk0, chunk0), (blk1, _), (blk2, chunk2) = lagged(0), lagged(1), lagged(2)
    route_spec = pl.BlockSpec((PEER_HEADS, PEER_N_KEYS, tm), lambda s: (0, 0, blk1(s)))
    body = functools.partial(_peer_mlp_body, blocks_per_step=te // PEER_N_KEYS, n_chunks=n_chunks, n_flat=n_flat)
    return pl.pallas_call(
        body, grid=(n_flat + PEER_PIPE_LAG,),
        in_specs=[pl.BlockSpec((d, tm), lambda s: (0, blk0(s))),
                  pl.BlockSpec((te, d), lambda s: (chunk0(s), 0)),
                  pl.BlockSpec((d, te), lambda s: (0, chunk2(s))),
                  route_spec, route_spec, route_spec, route_spec,
                  pl.BlockSpec((tm, d), lambda s: (blk2(s), 0)),
                  pl.BlockSpec((1, 1, d), lambda s: (seg(blk2(s)), 0, 0))],
        out_specs=pl.BlockSpec((tm, d), lambda s: (blk2(s), 0)),
        out_shape=jax.ShapeDtypeStruct((t, d), F32),
        scratch_shapes=[pltpu.VMEM((2, te, tm), F32), pltpu.VMEM((2, te, tm), BF16), pltpu.VMEM((d, tm), F32)],
        compiler_params=_cparams(("arbitrary",)), name="peer_experts",
    )(ht, u, vt, *routing, x, g_f)


def _rms_body(x_ref, g_ref, o_ref):
    xf = x_ref[...]
    o_ref[...] = xf * lax.rsqrt(jnp.mean(xf * xf, axis=-1, keepdims=True) + EPS) * g_ref[...]


def final_rms_norm(x, g, *, tm=1024):
    m, d = x.shape
    return pl.pallas_call(
        _rms_body, grid=(m // tm,),
        in_specs=[pl.BlockSpec((tm, d), lambda i: (i, 0)), pl.BlockSpec((1, d), lambda i: (0, 0))],
        out_specs=pl.BlockSpec((tm, d), lambda i: (i, 0)),
        out_shape=jax.ShapeDtypeStruct((m, d), F32),
        compiler_params=_cparams(("parallel",)), name="final_rms_norm")(x, g)


def _rope_tables(seq, extra_rows):
    quarter = DIFF_HEAD_DIM // 4
    t = jnp.arange(seq)
    row = (t // GRID_W).astype(F32)
    col = (t % GRID_W).astype(F32)
    inv = ROPE_BASE ** (-jnp.arange(quarter, dtype=F32) / quarter)
    cos_r, sin_r = jnp.cos(row[:, None] * inv), jnp.sin(row[:, None] * inv)
    cos_c, sin_c = jnp.cos(col[:, None] * inv), jnp.sin(col[:, None] * inv)
    cos64 = jnp.concatenate([cos_r, cos_r, cos_c, cos_c], axis=1)
    sin64 = jnp.concatenate([-sin_r, sin_r, -sin_c, sin_c], axis=1)
    cos = jnp.concatenate([cos64, cos64], axis=1)
    sin = jnp.concatenate([sin64, sin64], axis=1)
    cos = jnp.concatenate([cos, jnp.ones((extra_rows, LANES), F32)], axis=0)
    sin = jnp.concatenate([sin, jnp.zeros((extra_rows, LANES), F32)], axis=0)
    return cos, sin


def kernel(x, c, ctx, c_ctx, w_mod, b_mod, norm_mix_g, norm_ffn_g, even_w_in, even_w_out, na_rpb,
           odd_w_in, odd_w_out, diff_lambda_q1, diff_lambda_k1, diff_lambda_q2, diff_lambda_k2,
           diff_subln_g, peer_w_q, peer_sub_keys, peer_u, peer_v, final_norm_g):
    batch, seq, d = x.shape
    ctx_len = ctx.shape[1]
    depth = w_mod.shape[0]
    n_lat = batch * seq
    tm = 512
    xs = jnp.concatenate([x.reshape(n_lat, d), ctx.reshape(batch * ctx_len, d)], axis=0)
    cond = jnp.concatenate([c, c_ctx[None], jnp.zeros((SUBLANES - batch - 1, d), F32)], axis=0)
    cond = jax.nn.silu(cond)
    n_seg = batch + 1

    for layer in range(depth):
        last = layer == depth - 1
        jl = layer // 2
        mod = fused_matmul(cond, w_mod[layer].astype(BF16), tm=SUBLANES, tn=1024)[:n_seg] + b_mod[layer]
        sh_m, sc_m, g_m, sh_f, sc_f, g_f = [m_[:, None, :] for m_ in jnp.split(mod, 6, axis=-1)]
        norm_mix = (norm_mix_g[layer][None], sh_m, sc_m)
        norm_ffn = (norm_ffn_g[layer][None], sh_f, sc_f)
        if layer % 2 == 0:
            w_in = even_w_in[jl].astype(BF16)
            w_out = even_w_out[jl].astype(BF16)
            qkv = fused_matmul(xs, w_in[:, :3 * NA_WIDTH], norm=norm_mix, out_dtype=BF16, tm=tm)
            f = fused_matmul(xs, w_in[:, 3 * NA_WIDTH:], norm=norm_mix, tm=tm)
            attn = neighbourhood_attention(qkv, na_rpb[jl], batch=batch, seq=seq, ctx_len=ctx_len)
            four = fourier_long(f, batch=batch, seq=seq)
            if not last:
                attn = jnp.concatenate([attn, context_attention(qkv, batch=batch, seq=seq, ctx_len=ctx_len)], axis=0)
                four = jnp.concatenate([four, fourier_short(f, batch=batch, n=ctx_len, row_block0=n_lat // ctx_len)], axis=0)
            rows = attn.shape[0]
            xs_mix = fused_matmul(attn, w_out[:NA_WIDTH], res=(xs[:rows], g_m), tm=tm)
            xs_mix = fused_matmul(four, w_out[NA_WIDTH:], res=(xs_mix, g_m), tm=tm)
        else:
            lambda_init = 0.8 - 0.6 * math.exp(-0.3 * layer)
            lam = (jnp.exp(jnp.sum(diff_lambda_q1[jl] * diff_lambda_k1[jl]))
                   - jnp.exp(jnp.sum(diff_lambda_q2[jl] * diff_lambda_k2[jl])) + lambda_init).reshape(1)
            cos, sin = _rope_tables(seq, tm)
            v_width = DIFF_HEADS * 2 * DIFF_HEAD_DIM
            qkv, vt = fused_matmul(xs, odd_w_in[jl].astype(BF16), norm=norm_mix, out_dtype=BF16, tm=tm,
                                   rope=(cos, sin, 2 * DIFF_HEADS * 2 * DIFF_HEAD_DIM, seq // tm, n_lat // tm),
                                   t_cols=v_width)
            o = diff_attention(qkv, _with_ones_rows(vt, DIFF_HEADS), lam, diff_subln_g[jl][None], lambda_init,
                               batch=batch, seq=seq, ctx_len=ctx_len)
            if not last:
                raise NotImplementedError("context output of a differential-attention layer")
            xs_mix = fused_matmul(o, odd_w_out[jl].astype(BF16), res=(xs[:n_lat], g_m), tm=tm)
        q, ht = fused_matmul(xs_mix, peer_w_q[layer].astype(BF16), norm=norm_ffn, emit_ht=True, tm=tm)
        routing = peer_topk(q, peer_sub_keys[layer])
        xs = peer_experts(xs_mix, ht, routing, peer_u[layer].astype(BF16), peer_v[layer].astype(BF16).T, g_f)
        if not last and xs.shape[0] == n_lat:
            raise NotImplementedError("context stream dropped before the last layer")
    out = final_rms_norm(xs[:n_lat], final_norm_g[None])
    return out.reshape(batch, seq, d)
```

```python
import functools
import math

import jax
import jax.numpy as jnp
import numpy as np
from jax import lax
from jax.experimental import pallas as pl
from jax.experimental.pallas import tpu as pltpu

F32 = jnp.float32
BF16 = jnp.bfloat16

D_MODEL = 1024
GRID_W = 64
EPS = 1e-6
ROPE_BASE = 10000.0
NA_HEADS = 8
NA_HEAD_DIM = 64
NA_WIDTH = 512
NA_WIN_H = 8
NA_WIN_W = 16
FNET_GROUP_DIM = 128
DIFF_HEADS = 8
DIFF_HEAD_DIM = 64
PEER_HEADS = 8
PEER_N_KEYS = 128
PEER_TOPK = 16
LANES = 128
SUBLANES = 8
NEG = -1e30
VMEM_LIMIT = 56 * 1024 * 1024


def _cparams(sem, vmem=VMEM_LIMIT):
    return pltpu.CompilerParams(dimension_semantics=sem, vmem_limit_bytes=vmem)


def _mm_body(*refs, has_norm, emit_ht, rope_tiles, has_res, t_from):
    it = iter(refs)
    x_ref = next(it)
    if has_norm:
        g_ref, sh_ref, sc_ref = next(it), next(it), next(it)
    w_ref = next(it)
    if rope_tiles:
        cos_ref, sin_ref = next(it), next(it)
    if has_res:
        res_ref, gate_ref = next(it), next(it)
    o_ref = next(it)
    if emit_ht:
        ht_ref = next(it)
    if t_from is not None:
        ot_ref = next(it)
    if has_norm:
        xs_ref = next(it)
    j = pl.program_id(1)

    if has_norm:
        @pl.when(j == 0)
        def _():
            xf = x_ref[...]
            y = xf * lax.rsqrt(jnp.mean(xf * xf, axis=-1, keepdims=True) + EPS)
            y = y * g_ref[...] * (1.0 + sc_ref[0]) + sh_ref[0]
            xs_ref[...] = y.astype(BF16)
            if emit_ht:
                ht_ref[...] = y.T.astype(BF16)
        a = xs_ref[...]
    else:
        a = x_ref[...].astype(BF16)
    acc = jnp.dot(a, w_ref[...], preferred_element_type=F32)

    if t_from is not None:
        @pl.when(j >= t_from)
        def _():
            ot_ref[...] = acc.T.astype(ot_ref.dtype)

    def finish(v):
        if has_res:
            v = res_ref[...] + gate_ref[0] * v
        o_ref[...] = v.astype(o_ref.dtype)

    if rope_tiles:
        @pl.when(j < rope_tiles)
        def _():
            tn = acc.shape[1]
            reps = tn // LANES
            c = jnp.concatenate([cos_ref[...]] * reps, axis=1)
            s = jnp.concatenate([sin_ref[...]] * reps, axis=1)
            lane = lax.broadcasted_iota(jnp.int32, acc.shape, 1)
            first = (lane % 32) < 16
            partner = jnp.where(first, pltpu.roll(acc, tn - 16, 1), pltpu.roll(acc, 16, 1))
            finish(acc * c + partner * s)

        @pl.when(j >= rope_tiles)
        def _():
            finish(acc)
    else:
        finish(acc)


def _seg_of_block(i, tm, seg_rows, n_seg):
    return jnp.minimum((i * tm) // seg_rows, n_seg - 1)


def fused_matmul(x, w, *, norm=None, rope=None, res=None, emit_ht=False, t_cols=0, out_dtype=F32,
                 tm=512, tn=1024, seg_rows=8192):
    m, k = x.shape
    n = w.shape[1]
    tm = min(tm, m)
    tn = max(t for t in range(LANES, min(tn, n) + 1, LANES) if n % t == 0)
    assert m % tm == 0 and n % tn == 0
    grid = (m // tm, n // tn)
    in_specs = [pl.BlockSpec((tm, k), lambda i, j: (i, 0))]
    args = [x]
    n_seg = 1
    if norm is not None:
        g, sh, sc = norm
        n_seg = sh.shape[0]
        seg = functools.partial(_seg_of_block, tm=tm, seg_rows=seg_rows, n_seg=n_seg)
        in_specs += [pl.BlockSpec((1, k), lambda i, j: (0, 0)),
                     pl.BlockSpec((1, 1, k), lambda i, j: (seg(i), 0, 0)),
                     pl.BlockSpec((1, 1, k), lambda i, j: (seg(i), 0, 0))]
        args += [g, sh, sc]
    in_specs.append(pl.BlockSpec((k, tn), lambda i, j: (0, j)))
    args.append(w)
    rope_tiles = 0
    if rope is not None:
        cos, sin, n_rope_cols, n_pos_blocks, n_lat_blocks = rope
        rope_tiles = n_rope_cols // tn
        pos = lambda i, j: (jnp.where(i < n_lat_blocks, i % n_pos_blocks, n_pos_blocks), 0)
        in_specs += [pl.BlockSpec((tm, LANES), pos), pl.BlockSpec((tm, LANES), pos)]
        args += [cos, sin]
    if res is not None:
        r, gate = res
        n_seg_r = gate.shape[0]
        segr = functools.partial(_seg_of_block, tm=tm, seg_rows=seg_rows, n_seg=n_seg_r)
        in_specs += [pl.BlockSpec((tm, tn), lambda i, j: (i, j)),
                     pl.BlockSpec((1, 1, tn), lambda i, j: (segr(i), 0, j))]
        args += [r, gate]
    out_shape = [jax.ShapeDtypeStruct((m, n), out_dtype)]
    out_specs = [pl.BlockSpec((tm, tn), lambda i, j: (i, j))]
    if emit_ht:
        out_shape.append(jax.ShapeDtypeStruct((k, m), BF16))
        out_specs.append(pl.BlockSpec((k, tm), lambda i, j: (0, i)))
    t_from = None
    if t_cols:
        assert t_cols % tn == 0
        t_from = (n - t_cols) // tn
        out_shape.append(jax.ShapeDtypeStruct((t_cols, m), BF16))
        out_specs.append(pl.BlockSpec((tn, tm), lambda i, j: (jnp.maximum(j - t_from, 0), i)))
    scratch = [pltpu.VMEM((tm, k), BF16)] if norm is not None else []
    body = functools.partial(_mm_body, has_norm=norm is not None, emit_ht=emit_ht,
                             rope_tiles=rope_tiles, has_res=res is not None, t_from=t_from)
    outs = pl.pallas_call(
        body, grid=grid, in_specs=in_specs, out_specs=out_specs, out_shape=out_shape,
        scratch_shapes=scratch, compiler_params=_cparams(("parallel", "arbitrary")),
        name="fused_matmul")(*args)
    return outs if len(outs) > 1 else outs[0]


NA_GROUP = 4
NA_UNION = NA_WIN_H + NA_GROUP


def _na_body(q_ref, k_ref, v_ref, kc_ref, vc_ref, bias_ref, o_ref, *, rows):
    nq = NA_GROUP * GRID_W
    nkeys = NA_UNION * GRID_W
    lane = lax.broadcasted_iota(jnp.int32, (nq, LANES), 1)
    head_masks = [lane < NA_HEAD_DIM, lane >= NA_HEAD_DIM]
    kc = kc_ref[...]
    vc = vc_ref[...]
    scale = NA_HEAD_DIM ** -0.5
    nt = (((1,), (1,)), ((), ()))
    n_groups = rows // NA_GROUP

    def group_body(g, carry):
        r0 = g * NA_GROUP
        u0 = jnp.clip(r0 - NA_WIN_H // 2, 0, rows - NA_UNION)
        pattern = jnp.where(g == 0, 0, jnp.where(g == n_groups - 1, 2, 1))
        q_rows = pl.ds(pl.multiple_of(r0 * GRID_W, nq), nq)
        q = q_ref[q_rows, :]
        k_start = pl.multiple_of(u0 * GRID_W, NA_GROUP * GRID_W)
        kw = k_ref[pl.ds(k_start, nkeys), :]
        vw = v_ref[pl.ds(k_start, nkeys), :]
        outs = []
        for hh in range(2):
            qm = jnp.where(head_masks[hh], q, jnp.zeros_like(q))
            bias = bias_ref[pattern, :, hh].reshape(nq, nkeys)
            s_nb = lax.dot_general(qm, kw, nt, preferred_element_type=F32) * scale + bias
            s_cx = lax.dot_general(qm, kc, nt, preferred_element_type=F32) * scale
            m = jnp.maximum(jnp.max(s_nb, axis=1, keepdims=True), jnp.max(s_cx, axis=1, keepdims=True))
            p_nb = jnp.exp(s_nb - m)
            p_cx = jnp.exp(s_cx - m)
            l = jnp.sum(p_nb, axis=1, keepdims=True) + jnp.sum(p_cx, axis=1, keepdims=True)
            o = (jnp.dot(p_nb.astype(BF16), vw, preferred_element_type=F32)
                 + jnp.dot(p_cx.astype(BF16), vc, preferred_element_type=F32))
            outs.append(o / l)
        o_ref[q_rows, :] = jnp.where(head_masks[0], outs[0], outs[1]).astype(o_ref.dtype)
        return carry

    lax.fori_loop(0, n_groups, group_body, 0)


def _na_bias_table(rpb):
    col = np.arange(GRID_W)
    col_start = np.clip(col - NA_WIN_W // 2, 0, GRID_W - NA_WIN_W)
    kc = np.arange(GRID_W)
    inside = (kc[None, :] >= col_start[:, None]) & (kc[None, :] < col_start[:, None] + NA_WIN_W)
    dc = np.clip(kc[None, :] - col[:, None] + NA_WIN_W - 1, 0, 2 * NA_WIN_W - 2)
    per_dr = jnp.where(inside[None, None], rpb[:, :, dc], NEG)
    masked = jnp.full((NA_HEADS, GRID_W, GRID_W), NEG, F32)
    half = NA_WIN_H // 2
    patterns = [(0, lambda dr: -dr), (-half, lambda dr: -half), (-NA_WIN_H, lambda dr: -half - dr)]
    tabs = []
    for union_off, win_start in patterns:
        per_row = []
        for dr in range(NA_GROUP):
            blocks = []
            for kk in range(NA_UNION):
                delta = union_off + kk - dr
                ok = win_start(dr) <= delta < win_start(dr) + NA_WIN_H
                blocks.append(per_dr[:, delta + NA_WIN_H - 1] if ok else masked)
            per_row.append(jnp.concatenate(blocks, axis=-1))
        tabs.append(jnp.stack(per_row))
    return jnp.stack(tabs)


def neighbourhood_attention(qkv, rpb, *, batch, seq, ctx_len):
    rows = seq // GRID_W
    assert rows % NA_GROUP == 0 and rows >= NA_UNION + NA_GROUP
    bias = _na_bias_table(rpb)
    hp = NA_HEADS // 2
    ctx_blk0 = (batch * seq) // ctx_len
    body = functools.partial(_na_body, rows=rows)
    return pl.pallas_call(
        body, grid=(batch, hp),
        in_specs=[pl.BlockSpec((seq, LANES), lambda b, h: (b, h)),
                  pl.BlockSpec((seq, LANES), lambda b, h: (b, hp + h)),
                  pl.BlockSpec((seq, LANES), lambda b, h: (b, 2 * hp + h)),
                  pl.BlockSpec((ctx_len, LANES), lambda b, h: (ctx_blk0 + b, hp + h)),
                  pl.BlockSpec((ctx_len, LANES), lambda b, h: (ctx_blk0 + b, 2 * hp + h)),
                  pl.BlockSpec((3, NA_GROUP, 2, GRID_W, NA_UNION * GRID_W), lambda b, h: (0, 0, h, 0, 0))],
        out_specs=pl.BlockSpec((seq, LANES), lambda b, h: (b, h)),
        out_shape=jax.ShapeDtypeStruct((batch * seq, NA_WIDTH), BF16),
        compiler_params=_cparams(("parallel", "parallel")), name="neighbourhood_attention",
    )(qkv, qkv, qkv, qkv, qkv, bias)


def _ctx_attn_body(q_ref, k_ref, v_ref, o_ref):
    lane = lax.broadcasted_iota(jnp.int32, q_ref.shape, 1)
    q = q_ref[...]
    k = k_ref[...]
    v = v_ref[...]
    scale = NA_HEAD_DIM ** -0.5
    outs = []
    for hh in range(2):
        msk = (lane < NA_HEAD_DIM) if hh == 0 else (lane >= NA_HEAD_DIM)
        qm = jnp.where(msk, q, jnp.zeros_like(q))
        s = lax.dot_general(qm, k, (((1,), (1,)), ((), ())), preferred_element_type=F32) * scale
        m = jnp.max(s, axis=1, keepdims=True)
        p = jnp.exp(s - m)
        l = jnp.sum(p, axis=1, keepdims=True)
        outs.append(jnp.dot(p.astype(BF16), v, preferred_element_type=F32) / l)
    o_ref[...] = jnp.where(lane < NA_HEAD_DIM, outs[0], outs[1]).astype(o_ref.dtype)


def context_attention(qkv, *, batch, seq, ctx_len):
    hp = NA_HEADS // 2
    ctx_blk0 = (batch * seq) // ctx_len
    return pl.pallas_call(
        _ctx_attn_body, grid=(batch, hp),
        in_specs=[pl.BlockSpec((ctx_len, LANES), lambda b, h: (ctx_blk0 + b, h)),
                  pl.BlockSpec((ctx_len, LANES), lambda b, h: (ctx_blk0 + b, hp + h)),
                  pl.BlockSpec((ctx_len, LANES), lambda b, h: (ctx_blk0 + b, 2 * hp + h))],
        out_specs=pl.BlockSpec((ctx_len, LANES), lambda b, h: (b, h)),
        out_shape=jax.ShapeDtypeStruct((batch * ctx_len, NA_WIDTH), BF16),
        compiler_params=_cparams(("parallel", "parallel")), name="context_attention",
    )(qkv, qkv, qkv)


FFT_UNROLL = 4


def _dft_cs(n):
    k = np.arange(n)
    ang = 2.0 * np.pi * ((k[:, None] * k[None, :]) % n) / n
    return np.cos(ang), np.sin(ang)


def _fourier_long_body(f_ref, wch_ref, ma_ref, mb_ref, o_ref, pr_ref, pi_ref, z_ref, *, n1, n2):
    c = FNET_GROUP_DIM
    p = jnp.dot(f_ref[...].astype(BF16), wch_ref[...], preferred_element_type=F32)
    pr_ref[...] = p[:, :c]
    pi_ref[...] = p[:, c:]

    def stage_a(jj, carry):
        for t in range(FFT_UNROLL):
            j = jj * FFT_UNROLL + t
            rows = pl.ds(j, n1, stride=n2)
            x = jnp.concatenate([pr_ref[rows, :], pi_ref[rows, :]], axis=1).astype(BF16)
            y = jnp.dot(ma_ref[j], x, preferred_element_type=F32)
            zr = y[:n1, :c] - y[n1:, c:]
            zi = y[n1:, :c] + y[:n1, c:]
            z_ref[pl.ds(j, n1, stride=2 * n2), :] = zr
            z_ref[pl.ds(n2 + j, n1, stride=2 * n2), :] = zi
        return carry

    lax.fori_loop(0, n2 // FFT_UNROLL, stage_a, 0)

    def stage_b(kk, carry):
        for t in range(FFT_UNROLL):
            k1 = kk * FFT_UNROLL + t
            z = z_ref[pl.ds(pl.multiple_of(k1 * 2 * n2, 2 * n2), 2 * n2), :].astype(BF16)
            y = jnp.dot(mb_ref[...], z, preferred_element_type=F32)
            o_ref[pl.ds(k1, n2, stride=n1), :] = y.astype(o_ref.dtype)
        return carry

    lax.fori_loop(0, n1 // FFT_UNROLL, stage_b, 0)


def fourier_long(f, *, batch, seq):
    c = FNET_GROUP_DIM
    groups = f.shape[1] // c
    n2 = 128
    n1 = seq // n2
    norm = 1.0 / math.sqrt(seq * c)
    cc, sc = _dft_cs(c)
    wch = np.concatenate([cc, -sc], axis=1)
    c1, s1 = _dft_cs(n1)
    j = np.arange(n2)
    k1 = np.arange(n1)
    tw = 2.0 * np.pi * (j[:, None] * k1[None, :]) / seq
    ar = np.cos(tw)[:, :, None] * c1[None] - np.sin(tw)[:, :, None] * s1[None]
    ai = -(np.cos(tw)[:, :, None] * s1[None] + np.sin(tw)[:, :, None] * c1[None])
    ma = np.concatenate([ar, ai], axis=1)
    c2, s2 = _dft_cs(n2)
    mb = np.concatenate([c2, s2], axis=1) * norm
    body = functools.partial(_fourier_long_body, n1=n1, n2=n2)
    return pl.pallas_call(
        body, grid=(batch, groups),
        in_specs=[pl.BlockSpec((seq, c), lambda b, g: (b, g)),
                  pl.BlockSpec((c, 2 * c), lambda b, g: (0, 0)),
                  pl.BlockSpec((n2, 2 * n1, n1), lambda b, g: (0, 0, 0)),
                  pl.BlockSpec((n2, 2 * n2), lambda b, g: (0, 0))],
        out_specs=pl.BlockSpec((seq, c), lambda b, g: (b, g)),
        out_shape=jax.ShapeDtypeStruct((batch * seq, groups * c), F32),
        scratch_shapes=[pltpu.VMEM((seq, c), F32), pltpu.VMEM((seq, c), F32), pltpu.VMEM((2 * seq, c), F32)],
        compiler_params=_cparams(("parallel", "parallel")), name="fourier_long",
    )(f, jnp.asarray(wch, BF16), jnp.asarray(ma, BF16), jnp.asarray(mb, BF16))


def _fourier_short_body(f_ref, wch_ref, mp_ref, o_ref):
    c = FNET_GROUP_DIM
    p = jnp.dot(f_ref[...].astype(BF16), wch_ref[...], preferred_element_type=F32)
    z = jnp.concatenate([p[:, :c], p[:, c:]], axis=0).astype(BF16)
    o_ref[...] = jnp.dot(mp_ref[...], z, preferred_element_type=F32).astype(o_ref.dtype)


def fourier_short(f, *, batch, n, row_block0):
    c = FNET_GROUP_DIM
    groups = f.shape[1] // c
    norm = 1.0 / math.sqrt(n * c)
    cc, sc = _dft_cs(c)
    wch = np.concatenate([cc, -sc], axis=1)
    cn, sn = _dft_cs(n)
    mp = np.concatenate([cn, sn], axis=1) * norm
    return pl.pallas_call(
        _fourier_short_body, grid=(batch, groups),
        in_specs=[pl.BlockSpec((n, c), lambda b, g: (row_block0 + b, g)),
                  pl.BlockSpec((c, 2 * c), lambda b, g: (0, 0)),
                  pl.BlockSpec((n, 2 * n), lambda b, g: (0, 0))],
        out_specs=pl.BlockSpec((n, c), lambda b, g: (b, g)),
        out_shape=jax.ShapeDtypeStruct((batch * n, groups * c), F32),
        compiler_params=_cparams(("parallel", "parallel")), name="fourier_short",
    )(f, jnp.asarray(wch, BF16), jnp.asarray(mp, BF16))


LOG2E = 1.4426950408889634


ONES_ROWS = 16


def _diff_body(lam_ref, q_ref, kl_ref, kn_ref, vtl_ref, kc_ref, vtc_ref, g_ref, o_ref,
               qs_ref, m_ref, acc_ref, st0_ref, *, nk, out_scale):
    j = pl.program_id(3)
    d = DIFF_HEAD_DIM

    def scores(k, comp):
        return lax.dot_general(k, qs_ref[comp], (((1,), (1,)), ((), ())),
                               preferred_element_type=F32)

    @pl.when(j == 0)
    def _():
        q = q_ref[...].astype(F32) * (d ** -0.5 * LOG2E)
        lane = lax.broadcasted_iota(jnp.int32, q.shape, 1)
        qs_ref[0] = jnp.where(lane < d, q, 0.0).astype(BF16)
        qs_ref[1] = jnp.where(lane >= d, q, 0.0).astype(BF16)
        m_ref[...] = jnp.full(m_ref.shape, NEG, F32)
        acc_ref[...] = jnp.zeros(acc_ref.shape, F32)
        st0_ref[...] = scores(kl_ref[...], 0)

    def softmax_pv(comp, st, vt):
        m_old = m_ref[comp]
        m_new = jnp.maximum(m_old, jnp.max(st, axis=0, keepdims=True))
        alpha = jnp.exp2(m_old - m_new)
        p = jnp.exp2((st - m_new).astype(BF16))
        acc_ref[comp] = alpha * acc_ref[comp] + jnp.dot(vt, p, preferred_element_type=F32)
        m_ref[comp] = m_new

    def step(k, vt, k_next):
        st1 = scores(k, 1)
        softmax_pv(0, st0_ref[:k.shape[0], :], vt)
        if k_next is not None:
            st0_ref[:k_next.shape[0], :] = scores(k_next, 0)
        softmax_pv(1, st1, vt)

    @pl.when(j < nk - 1)
    def _():
        step(kl_ref[...], vtl_ref[...], kn_ref[...])

    @pl.when(j == nk - 1)
    def _():
        step(kl_ref[...], vtl_ref[...], kc_ref[...])

    @pl.when(j == nk)
    def _():
        step(kc_ref[...], vtc_ref[...], None)
        lam = lam_ref[0]
        n = 2 * d
        o0 = acc_ref[0, :n, :] / acc_ref[0, n:n + 1, :]
        o1 = acc_ref[1, :n, :] / acc_ref[1, n:n + 1, :]
        ot = o0 - lam * o1
        yt = ot * lax.rsqrt(jnp.mean(ot * ot, axis=0, keepdims=True) + EPS)
        o_ref[...] = (yt.T * g_ref[...] * out_scale).astype(o_ref.dtype)


def diff_attention(qkv, vt, lam, subln_g, lambda_init, *, batch, seq, ctx_len, tq=1024, tk=2048):
    h = DIFF_HEADS
    nq = seq // tq
    nk = seq // tk
    vrows = 2 * DIFF_HEAD_DIM + ONES_ROWS
    ctx_blk0 = (batch * seq) // ctx_len
    body = functools.partial(_diff_body, nk=nk, out_scale=1.0 - lambda_init)
    assert nk >= 2 and ctx_len <= tk
    lat = lambda b, hh, i, j: b * nk + jnp.minimum(j, nk - 1)
    nxt = lambda b, hh, i, j: b * nk + jnp.minimum(j + 1, nk - 1)
    return pl.pallas_call(
        body, grid=(batch, h, nq, nk + 1),
        in_specs=[pl.BlockSpec(memory_space=pltpu.SMEM),
                  pl.BlockSpec((tq, LANES), lambda b, hh, i, j: (b * nq + i, hh)),
                  pl.BlockSpec((tk, LANES), lambda b, hh, i, j: (lat(b, hh, i, j), h + hh)),
                  pl.BlockSpec((tk, LANES), lambda b, hh, i, j: (nxt(b, hh, i, j), h + hh)),
                  pl.BlockSpec((vrows, tk), lambda b, hh, i, j: (hh, lat(b, hh, i, j))),
                  pl.BlockSpec((ctx_len, LANES), lambda b, hh, i, j: (ctx_blk0 + b, h + hh)),
                  pl.BlockSpec((vrows, ctx_len), lambda b, hh, i, j: (hh, ctx_blk0 + b)),
                  pl.BlockSpec((1, LANES), lambda b, hh, i, j: (0, 0))],
        out_specs=pl.BlockSpec((tq, LANES), lambda b, hh, i, j: (b * nq + i, hh)),
        out_shape=jax.ShapeDtypeStruct((batch * seq, h * LANES), BF16),
        scratch_shapes=[pltpu.VMEM((2, tq, LANES), BF16), pltpu.VMEM((2, 1, tq), F32),
                        pltpu.VMEM((2, vrows, tq), F32), pltpu.VMEM((tk, tq), F32)],
        compiler_params=_cparams(("parallel", "parallel", "parallel", "arbitrary")),
        name="diff_attention",
    )(lam, qkv, qkv, qkv, vt, qkv, vt, subln_g)


def _with_ones_rows(vt, heads):
    hw, rows = vt.shape
    blocks = vt.reshape(heads, hw // heads, rows)
    ones = jnp.ones((heads, ONES_ROWS, rows), vt.dtype)
    return jnp.concatenate([blocks, ones], axis=1).reshape(hw + heads * ONES_ROWS, rows)


def _extract16(s, pos):
    big = float(2 ** 20)
    rank = jnp.full(s.shape, float(PEER_TOPK), F32)
    vals = []
    for r in range(PEER_TOPK):
        m = jnp.max(s, axis=0, keepdims=True)
        sel = jnp.min(jnp.where(s == m, pos, big), axis=0, keepdims=True)
        hit = pos == sel
        vals.append(m)
        rank = jnp.where(hit, float(r), rank)
        s = jnp.where(hit, -jnp.inf, s)
    return vals, rank, rank < float(PEER_TOPK)


_CAND_GROUPS = ((0, 16), (1, 8), (2, 5), (3, 4), (4, 3), (5, 2), (6, 2), (7, 2))


def _peer_topk_body(q_ref, keys_ref, r1_ref, e1_ref, n0_ref, c0_ref):
    q = q_ref[...]
    kd = q.shape[1] // 2
    tb = q.shape[0]
    res = []
    for half in range(2):
        s = lax.dot_general(keys_ref[0, half], q[:, half * kd:(half + 1) * kd],
                            (((1,), (1,)), ((), ())), preferred_element_type=F32)
        pos = lax.broadcasted_iota(jnp.int32, s.shape, 0).astype(F32)
        vals, rank, _ = _extract16(s, pos)
        res.append((s, vals, rank))
    (s0, v0, rank0), (s1, v1, rank1) = res
    v1_lo = jnp.concatenate(v1[:SUBLANES], axis=0)
    v1_hi = jnp.concatenate(v1[SUBLANES:], axis=0)
    row8 = lax.broadcasted_iota(jnp.int32, (SUBLANES, tb), 0)
    pieces, poses = [], []
    for a, nb in _CAND_GROUPS:
        if nb > SUBLANES:
            pieces += [v0[a] + v1_lo, v0[a] + v1_hi]
            poses += [a * PEER_TOPK + row8, a * PEER_TOPK + SUBLANES + row8]
        else:
            pieces.append(jnp.where(row8 < nb, v0[a] + v1_lo, -jnp.inf))
            poses.append(a * PEER_TOPK + row8)
    pieces.append(jnp.concatenate(v0[SUBLANES:], axis=0) + v1[0])
    poses.append((row8 + SUBLANES) * PEER_TOPK)
    cand = jnp.concatenate(pieces, axis=0)
    cpos = jnp.concatenate(poses, axis=0).astype(F32)
    best, _, taken = _extract16(cand, cpos)
    z = sum(jnp.exp(b - best[0]) for b in best)
    takenf = taken.astype(F32)
    counts = []
    row = 0
    for a, nb in _CAND_GROUPS:
        rows = 2 * SUBLANES if nb > SUBLANES else SUBLANES
        counts.append(jnp.sum(takenf[row:row + rows], axis=0, keepdims=True))
        row += rows
    counts += [takenf[row + i:row + i + 1] for i in range(SUBLANES)]
    n0 = jnp.zeros(s0.shape, F32)
    for a in range(PEER_TOPK):
        n0 = jnp.where(rank0 == float(a), counts[a], n0)
    r1_ref[0] = rank1.astype(r1_ref.dtype)
    e1_ref[0] = jnp.exp(s1 - v1[0]).astype(e1_ref.dtype)
    n0_ref[0] = n0
    c0_ref[0] = jnp.exp(s0 - v0[0]) / z


def peer_topk(q, sub_keys, *, tb=512):
    t = q.shape[0]
    h = PEER_HEADS
    spec = pl.BlockSpec((1, PEER_N_KEYS, tb), lambda i, hh: (hh, 0, i))
    shapes = [jax.ShapeDtypeStruct((h, PEER_N_KEYS, t), dt) for dt in (BF16, BF16, F32, F32)]
    return pl.pallas_call(
        _peer_topk_body, grid=(t // tb, h),
        in_specs=[pl.BlockSpec((tb, 2 * LANES), lambda i, hh: (i, hh)),
                  pl.BlockSpec((1, 2, PEER_N_KEYS, LANES), lambda i, hh: (hh, 0, 0, 0))],
        out_specs=[spec] * 4, out_shape=shapes,
        compiler_params=_cparams(("parallel", "parallel")), name="peer_topk",
    )(q, sub_keys)


PEER_PIPE_LAG = 2
GELU_C1 = math.sqrt(2.0 / math.pi)
GELU_C2 = 0.044715 * GELU_C1


def _peer_mlp_body(ht_ref, u_ref, vt_ref, r1_ref, e1_ref, n0_ref, c0_ref, x_ref, gate_ref, o_ref,
                   at_ref, wt_ref, acc_ref, *, blocks_per_step, n_chunks, n_flat):
    s = pl.program_id(0)
    cur = s % 2
    prev = 1 - cur
    c1 = jnp.clip(s - 1, 0, n_flat - 1) % n_chunks
    c2 = jnp.clip(s - 2, 0, n_flat - 1) % n_chunks

    @pl.when(s == 0)
    def _():
        at_ref[...] = jnp.zeros(at_ref.shape, F32)
        wt_ref[...] = jnp.zeros(wt_ref.shape, BF16)

    @pl.when(c2 == 0)
    def _():
        acc_ref[...] = jnp.zeros(acc_ref.shape, F32)

    acc_ref[...] += jnp.dot(vt_ref[...], wt_ref[prev], preferred_element_type=F32)

    tiles = []
    for ib in range(blocks_per_step):
        i = c1 * blocks_per_step + ib
        g = None
        for h in range(PEER_HEADS):
            n0 = n0_ref[h, pl.ds(i, 1), :].astype(BF16)
            c0 = c0_ref[h, pl.ds(i, 1), :].astype(BF16)
            term = jnp.where(r1_ref[h] < n0, e1_ref[h], jnp.zeros((), BF16)) * c0
            g = term if g is None else g + term
        tiles.append(g)
    gt = jnp.concatenate(tiles, axis=0)
    a = at_ref[prev].astype(BF16)
    inner = a * (GELU_C1 + GELU_C2 * (a * a))
    half = 0.5 * a
    wt_ref[cur] = (half + half * jnp.tanh(inner)) * gt

    at_ref[cur] = jnp.dot(u_ref[...], ht_ref[...], preferred_element_type=F32)

    @pl.when((c2 == n_chunks - 1) & (s >= PEER_PIPE_LAG))
    def _():
        o_ref[...] = x_ref[...] + gate_ref[0] * acc_ref[...].T


def peer_experts(x, ht, routing, u, vt, g_f, *, tm=512, te=1024, seg_rows=8192):
    t, d = x.shape
    n_exp = u.shape[0]
    n_seg = g_f.shape[0]
    n_chunks = n_exp // te
    n_flat = (t // tm) * n_chunks
    seg = functools.partial(_seg_of_block, tm=tm, seg_rows=seg_rows, n_seg=n_seg)

    def lagged(lag):
        f = lambda s: jnp.clip(s - lag, 0, n_flat - 1)
        return (lambda s: f(s) // n_chunks), (lambda s: f(s) % n_chunks)

    (blk0, chunk0), (blk1, _), (blk2, chunk2) = lagged(0), lagged(1), lagged(2)
    route_spec = pl.BlockSpec((PEER_HEADS, PEER_N_KEYS, tm), lambda s: (0, 0, blk1(s)))
    body = functools.partial(_peer_mlp_body, blocks_per_step=te // PEER_N_KEYS, n_chunks=n_chunks, n_flat=n_flat)
    return pl.pallas_call(
        body, grid=(n_flat + PEER_PIPE_LAG,),
        in_specs=[pl.BlockSpec((d, tm), lambda s: (0, blk0(s))),
                  pl.BlockSpec((te, d), lambda s: (chunk0(s), 0)),
                  pl.BlockSpec((d, te), lambda s: (0, chunk2(s))),
                  route_spec, route_spec, route_spec, route_spec,
                  pl.BlockSpec((tm, d), lambda s: (blk2(s), 0)),
                  pl.BlockSpec((1, 1, d), lambda s: (seg(blk2(s)), 0, 0))],
        out_specs=pl.BlockSpec((tm, d), lambda s: (blk2(s), 0)),
        out_shape=jax.ShapeDtypeStruct((t, d), F32),
        scratch_shapes=[pltpu.VMEM((2, te, tm), F32), pltpu.VMEM((2, te, tm), BF16), pltpu.VMEM((d, tm), F32)],
        compiler_params=_cparams(("arbitrary",)), name="peer_experts",
    )(ht, u, vt, *routing, x, g_f)


def _rms_body(x_ref, g_ref, o_ref):
    xf = x_ref[...]
    o_ref[...] = xf * lax.rsqrt(jnp.mean(xf * xf, axis=-1, keepdims=True) + EPS) * g_ref[...]


def final_rms_norm(x, g, *, tm=1024):
    m, d = x.shape
    return pl.pallas_call(
        _rms_body, grid=(m // tm,),
        in_specs=[pl.BlockSpec((tm, d), lambda i: (i, 0)), pl.BlockSpec((1, d), lambda i: (0, 0))],
        out_specs=pl.BlockSpec((tm, d), lambda i: (i, 0)),
        out_shape=jax.ShapeDtypeStruct((m, d), F32),
        compiler_params=_cparams(("parallel",)), name="final_rms_norm")(x, g)


def _rope_tables(seq, extra_rows):
    quarter = DIFF_HEAD_DIM // 4
    t = jnp.arange(seq)
    row = (t // GRID_W).astype(F32)
    col = (t % GRID_W).astype(F32)
    inv = ROPE_BASE ** (-jnp.arange(quarter, dtype=F32) / quarter)
    cos_r, sin_r = jnp.cos(row[:, None] * inv), jnp.sin(row[:, None] * inv)
    cos_c, sin_c = jnp.cos(col[:, None] * inv), jnp.sin(col[:, None] * inv)
    cos64 = jnp.concatenate([cos_r, cos_r, cos_c, cos_c], axis=1)
    sin64 = jnp.concatenate([-sin_r, sin_r, -sin_c, sin_c], axis=1)
    cos = jnp.concatenate([cos64, cos64], axis=1)
    sin = jnp.concatenate([sin64, sin64], axis=1)
    cos = jnp.concatenate([cos, jnp.ones((extra_rows, LANES), F32)], axis=0)
    sin = jnp.concatenate([sin, jnp.zeros((extra_rows, LANES), F32)], axis=0)
    return cos, sin


def kernel(x, c, ctx, c_ctx, w_mod, b_mod, norm_mix_g, norm_ffn_g, even_w_in, even_w_out, na_rpb,
           odd_w_in, odd_w_out, diff_lambda_q1, diff_lambda_k1, diff_lambda_q2, diff_lambda_k2,
           diff_subln_g, peer_w_q, peer_sub_keys, peer_u, peer_v, final_norm_g):
    batch, seq, d = x.shape
    ctx_len = ctx.shape[1]
    depth = w_mod.shape[0]
    n_lat = batch * seq
    tm = 512
    xs = jnp.concatenate([x.reshape(n_lat, d), ctx.reshape(batch * ctx_len, d)], axis=0)
    cond = jnp.concatenate([c, c_ctx[None], jnp.zeros((SUBLANES - batch - 1, d), F32)], axis=0)
    cond = jax.nn.silu(cond)
    n_seg = batch + 1

    for layer in range(depth):
        last = layer == depth - 1
        jl = layer // 2
        mod = fused_matmul(cond, w_mod[layer].astype(BF16), tm=SUBLANES, tn=1024)[:n_seg] + b_mod[layer]
        sh_m, sc_m, g_m, sh_f, sc_f, g_f = [m_[:, None, :] for m_ in jnp.split(mod, 6, axis=-1)]
        norm_mix = (norm_mix_g[layer][None], sh_m, sc_m)
        norm_ffn = (norm_ffn_g[layer][None], sh_f, sc_f)
        if layer % 2 == 0:
            w_in = even_w_in[jl].astype(BF16)
            w_out = even_w_out[jl].astype(BF16)
            qkv = fused_matmul(xs, w_in[:, :3 * NA_WIDTH], norm=norm_mix, out_dtype=BF16, tm=tm)
            f = fused_matmul(xs, w_in[:, 3 * NA_WIDTH:], norm=norm_mix, tm=tm)
            attn = neighbourhood_attention(qkv, na_rpb[jl], batch=batch, seq=seq, ctx_len=ctx_len)
            four = fourier_long(f, batch=batch, seq=seq)
            if not last:
                attn = jnp.concatenate([attn, context_attention(qkv, batch=batch, seq=seq, ctx_len=ctx_len)], axis=0)
                four = jnp.concatenate([four, fourier_short(f, batch=batch, n=ctx_len, row_block0=n_lat // ctx_len)], axis=0)
            rows = attn.shape[0]
            xs_mix = fused_matmul(attn, w_out[:NA_WIDTH], res=(xs[:rows], g_m), tm=tm)
            xs_mix = fused_matmul(four, w_out[NA_WIDTH:], res=(xs_mix, g_m), tm=tm)
        else:
            lambda_init = 0.8 - 0.6 * math.exp(-0.3 * layer)
            lam = (jnp.exp(jnp.sum(diff_lambda_q1[jl] * diff_lambda_k1[jl]))
                   - jnp.exp(jnp.sum(diff_lambda_q2[jl] * diff_lambda_k2[jl])) + lambda_init).reshape(1)
            cos, sin = _rope_tables(seq, tm)
            v_width = DIFF_HEADS * 2 * DIFF_HEAD_DIM
            qkv, vt = fused_matmul(xs, odd_w_in[jl].astype(BF16), norm=norm_mix, out_dtype=BF16, tm=tm,
                                   rope=(cos, sin, 2 * DIFF_HEADS * 2 * DIFF_HEAD_DIM, seq // tm, n_lat // tm),
                                   t_cols=v_width)
            o = diff_attention(qkv, _with_ones_rows(vt, DIFF_HEADS), lam, diff_subln_g[jl][None], lambda_init,
                               batch=batch, seq=seq, ctx_len=ctx_len)
            if not last:
                raise NotImplementedError("context output of a differential-attention layer")
            xs_mix = fused_matmul(o, odd_w_out[jl].astype(BF16), res=(xs[:n_lat], g_m), tm=tm)
        q, ht = fused_matmul(xs_mix, peer_w_q[layer].astype(BF16), norm=norm_ffn, emit_ht=True, tm=tm)
        routing = peer_topk(q, peer_sub_keys[layer])
        xs = peer_experts(xs_mix, ht, routing, peer_u[layer].astype(BF16), peer_v[layer].astype(BF16).T, g_f)
        if not last and xs.shape[0] == n_lat:
            raise NotImplementedError("context stream dropped before the last layer")
    out = final_rms_norm(xs[:n_lat], final_norm_g[None])
    return out.reshape(batch, seq, d)
```

```python
import functools
import math

import jax
import jax.numpy as jnp
import numpy as np
from jax import lax
from jax.experimental import pallas as pl
from jax.experimental.pallas import tpu as pltpu

F32 = jnp.float32
BF16 = jnp.bfloat16

D_MODEL = 1024
GRID_W = 64
EPS = 1e-6
ROPE_BASE = 10000.0
NA_HEADS = 8
NA_HEAD_DIM = 64
NA_WIDTH = 512
NA_WIN_H = 8
NA_WIN_W = 16
FNET_GROUP_DIM = 128
DIFF_HEADS = 8
DIFF_HEAD_DIM = 64
PEER_HEADS = 8
PEER_N_KEYS = 128
PEER_TOPK = 16
LANES = 128
SUBLANES = 8
NEG = -1e30
VMEM_LIMIT = 56 * 1024 * 1024


def _cparams(sem, vmem=VMEM_LIMIT):
    return pltpu.CompilerParams(dimension_semantics=sem, vmem_limit_bytes=vmem)


def _mm_body(*refs, has_norm, emit_ht, rope_tiles, has_res, t_from):
    it = iter(refs)
    x_ref = next(it)
    if has_norm:
        g_ref, sh_ref, sc_ref = next(it), next(it), next(it)
    w_ref = next(it)
    if rope_tiles:
        cos_ref, sin_ref = next(it), next(it)
    if has_res:
        res_ref, gate_ref = next(it), next(it)
    o_ref = next(it)
    if emit_ht:
        ht_ref = next(it)
    if t_from is not None:
        ot_ref = next(it)
    if has_norm:
        xs_ref = next(it)
    j = pl.program_id(1)

    if has_norm:
        @pl.when(j == 0)
        def _():
            xf = x_ref[...]
            y = xf * lax.rsqrt(jnp.mean(xf * xf, axis=-1, keepdims=True) + EPS)
            y = y * g_ref[...] * (1.0 + sc_ref[0]) + sh_ref[0]
            xs_ref[...] = y.astype(BF16)
            if emit_ht:
                ht_ref[...] = y.T.astype(BF16)
        a = xs_ref[...]
    else:
        a = x_ref[...].astype(BF16)
    acc = jnp.dot(a, w_ref[...], preferred_element_type=F32)

    if t_from is not None:
        @pl.when(j >= t_from)
        def _():
            ot_ref[...] = acc.T.astype(ot_ref.dtype)

    def finish(v):
        if has_res:
            v = res_ref[...] + gate_ref[0] * v
        o_ref[...] = v.astype(o_ref.dtype)

    if rope_tiles:
        @pl.when(j < rope_tiles)
        def _():
            tn = acc.shape[1]
            reps = tn // LANES
            c = jnp.concatenate([cos_ref[...]] * reps, axis=1)
            s = jnp.concatenate([sin_ref[...]] * reps, axis=1)
            lane = lax.broadcasted_iota(jnp.int32, acc.shape, 1)
            first = (lane % 32) < 16
            partner = jnp.where(first, pltpu.roll(acc, tn - 16, 1), pltpu.roll(acc, 16, 1))
            finish(acc * c + partner * s)

        @pl.when(j >= rope_tiles)
        def _():
            finish(acc)
    else:
        finish(acc)


def _seg_of_block(i, tm, seg_rows, n_seg):
    return jnp.minimum((i * tm) // seg_rows, n_seg - 1)


def fused_matmul(x, w, *, norm=None, rope=None, res=None, emit_ht=False, t_cols=0, out_dtype=F32,
                 tm=512, tn=1024, seg_rows=8192):
    m, k = x.shape
    n = w.shape[1]
    tm = min(tm, m)
    tn = max(t for t in range(LANES, min(tn, n) + 1, LANES) if n % t == 0)
    assert m % tm == 0 and n % tn == 0
    grid = (m // tm, n // tn)
    in_specs = [pl.BlockSpec((tm, k), lambda i, j: (i, 0))]
    args = [x]
    n_seg = 1
    if norm is not None:
        g, sh, sc = norm
        n_seg = sh.shape[0]
        seg = functools.partial(_seg_of_block, tm=tm, seg_rows=seg_rows, n_seg=n_seg)
        in_specs += [pl.BlockSpec((1, k), lambda i, j: (0, 0)),
                     pl.BlockSpec((1, 1, k), lambda i, j: (seg(i), 0, 0)),
                     pl.BlockSpec((1, 1, k), lambda i, j: (seg(i), 0, 0))]
        args += [g, sh, sc]
    in_specs.append(pl.BlockSpec((k, tn), lambda i, j: (0, j)))
    args.append(w)
    rope_tiles = 0
    if rope is not None:
        cos, sin, n_rope_cols, n_pos_blocks, n_lat_blocks = rope
        rope_tiles = n_rope_cols // tn
        pos = lambda i, j: (jnp.where(i < n_lat_blocks, i % n_pos_blocks, n_pos_blocks), 0)
        in_specs += [pl.BlockSpec((tm, LANES), pos), pl.BlockSpec((tm, LANES), pos)]
        args += [cos, sin]
    if res is not None:
        r, gate = res
        n_seg_r = gate.shape[0]
        segr = functools.partial(_seg_of_block, tm=tm, seg_rows=seg_rows, n_seg=n_seg_r)
        in_specs += [pl.BlockSpec((tm, tn), lambda i, j: (i, j)),
                     pl.BlockSpec((1, 1, tn), lambda i, j: (segr(i), 0, j))]
        args += [r, gate]
    out_shape = [jax.ShapeDtypeStruct((m, n), out_dtype)]
    out_specs = [pl.BlockSpec((tm, tn), lambda i, j: (i, j))]
    if emit_ht:
        out_shape.append(jax.ShapeDtypeStruct((k, m), BF16))
        out_specs.append(pl.BlockSpec((k, tm), lambda i, j: (0, i)))
    t_from = None
    if t_cols:
        assert t_cols % tn == 0
        t_from = (n - t_cols) // tn
        out_shape.append(jax.ShapeDtypeStruct((t_cols, m), BF16))
        out_specs.append(pl.BlockSpec((tn, tm), lambda i, j: (jnp.maximum(j - t_from, 0), i)))
    scratch = [pltpu.VMEM((tm, k), BF16)] if norm is not None else []
    body = functools.partial(_mm_body, has_norm=norm is not None, emit_ht=emit_ht,
                             rope_tiles=rope_tiles, has_res=res is not None, t_from=t_from)
    outs = pl.pallas_call(
        body, grid=grid, in_specs=in_specs, out_specs=out_specs, out_shape=out_shape,
        scratch_shapes=scratch, compiler_params=_cparams(("parallel", "arbitrary")),
        name="fused_matmul")(*args)
    return outs if len(outs) > 1 else outs[0]


NA_GROUP = 4
NA_UNION = NA_WIN_H + NA_GROUP


def _na_body(q_ref, k_ref, v_ref, kc_ref, vc_ref, bias_ref, o_ref, *, rows):
    nq = NA_GROUP * GRID_W
    nkeys = NA_UNION * GRID_W
    lane = lax.broadcasted_iota(jnp.int32, (nq, LANES), 1)
    head_masks = [lane < NA_HEAD_DIM, lane >= NA_HEAD_DIM]
    kc = kc_ref[...]
    vc = vc_ref[...]
    scale = NA_HEAD_DIM ** -0.5
    nt = (((1,), (1,)), ((), ()))
    n_groups = rows // NA_GROUP

    def group_body(g, carry):
        r0 = g * NA_GROUP
        u0 = jnp.clip(r0 - NA_WIN_H // 2, 0, rows - NA_UNION)
        pattern = jnp.where(g == 0, 0, jnp.where(g == n_groups - 1, 2, 1))
        q_rows = pl.ds(pl.multiple_of(r0 * GRID_W, nq), nq)
        q = q_ref[q_rows, :]
        k_start = pl.multiple_of(u0 * GRID_W, NA_GROUP * GRID_W)
        kw = k_ref[pl.ds(k_start, nkeys), :]
        vw = v_ref[pl.ds(k_start, nkeys), :]
        outs = []
        for hh in range(2):
            qm = jnp.where(head_masks[hh], q, jnp.zeros_like(q))
            bias = bias_ref[pattern, :, hh].reshape(nq, nkeys)
            s_nb = lax.dot_general(qm, kw, nt, preferred_element_type=F32) * scale + bias
            s_cx = lax.dot_general(qm, kc, nt, preferred_element_type=F32) * scale
            m = jnp.maximum(jnp.max(s_nb, axis=1, keepdims=True), jnp.max(s_cx, axis=1, keepdims=True))
            p_nb = jnp.exp(s_nb - m)
            p_cx = jnp.exp(s_cx - m)
            l = jnp.sum(p_nb, axis=1, keepdims=True) + jnp.sum(p_cx, axis=1, keepdims=True)
            o = (jnp.dot(p_nb.astype(BF16), vw, preferred_element_type=F32)
                 + jnp.dot(p_cx.astype(BF16), vc, preferred_element_type=F32))
            outs.append(o / l)
        o_ref[q_rows, :] = jnp.where(head_masks[0], outs[0], outs[1]).astype(o_ref.dtype)
        return carry

    lax.fori_loop(0, n_groups, group_body, 0)


def _na_bias_table(rpb):
    col = np.arange(GRID_W)
    col_start = np.clip(col - NA_WIN_W // 2, 0, GRID_W - NA_WIN_W)
    kc = np.arange(GRID_W)
    inside = (kc[None, :] >= col_start[:, None]) & (kc[None, :] < col_start[:, None] + NA_WIN_W)
    dc = np.clip(kc[None, :] - col[:, None] + NA_WIN_W - 1, 0, 2 * NA_WIN_W - 2)
    per_dr = jnp.where(inside[None, None], rpb[:, :, dc], NEG)
    masked = jnp.full((NA_HEADS, GRID_W, GRID_W), NEG, F32)
    half = NA_WIN_H // 2
    patterns = [(0, lambda dr: -dr), (-half, lambda dr: -half), (-NA_WIN_H, lambda dr: -half - dr)]
    tabs = []
    for union_off, win_start in patterns:
        per_row = []
        for dr in range(NA_GROUP):
            blocks = []
            for kk in range(NA_UNION):
                delta = union_off + kk - dr
                ok = win_start(dr) <= delta < win_start(dr) + NA_WIN_H
                blocks.append(per_dr[:, delta + NA_WIN_H - 1] if ok else masked)
            per_row.append(jnp.concatenate(blocks, axis=-1))
        tabs.append(jnp.stack(per_row))
    return jnp.stack(tabs)


def neighbourhood_attention(qkv, rpb, *, batch, seq, ctx_len):
    rows = seq // GRID_W
    assert rows % NA_GROUP == 0 and rows >= NA_UNION + NA_GROUP
    bias = _na_bias_table(rpb)
    hp = NA_HEADS // 2
    ctx_blk0 = (batch * seq) // ctx_len
    body = functools.partial(_na_body, rows=rows)
    return pl.pallas_call(
        body, grid=(batch, hp),
        in_specs=[pl.BlockSpec((seq, LANES), lambda b, h: (b, h)),
                  pl.BlockSpec((seq, LANES), lambda b, h: (b, hp + h)),
                  pl.BlockSpec((seq, LANES), lambda b, h: (b, 2 * hp + h)),
                  pl.BlockSpec((ctx_len, LANES), lambda b, h: (ctx_blk0 + b, hp + h)),
                  pl.BlockSpec((ctx_len, LANES), lambda b, h: (ctx_blk0 + b, 2 * hp + h)),
                  pl.BlockSpec((3, NA_GROUP, 2, GRID_W, NA_UNION * GRID_W), lambda b, h: (0, 0, h, 0, 0))],
        out_specs=pl.BlockSpec((seq, LANES), lambda b, h: (b, h)),
        out_shape=jax.ShapeDtypeStruct((batch * seq, NA_WIDTH), BF16),
        compiler_params=_cparams(("parallel", "parallel")), name="neighbourhood_attention",
    )(qkv, qkv, qkv, qkv, qkv, bias)


def _ctx_attn_body(q_ref, k_ref, v_ref, o_ref):
    lane = lax.broadcasted_iota(jnp.int32, q_ref.shape, 1)
    q = q_ref[...]
    k = k_ref[...]
    v = v_ref[...]
    scale = NA_HEAD_DIM ** -0.5
    outs = []
    for hh in range(2):
        msk = (lane < NA_HEAD_DIM) if hh == 0 else (lane >= NA_HEAD_DIM)
        qm = jnp.where(msk, q, jnp.zeros_like(q))
        s = lax.dot_general(qm, k, (((1,), (1,)), ((), ())), preferred_element_type=F32) * scale
        m = jnp.max(s, axis=1, keepdims=True)
        p = jnp.exp(s - m)
        l = jnp.sum(p, axis=1, keepdims=True)
        outs.append(jnp.dot(p.astype(BF16), v, preferred_element_type=F32) / l)
    o_ref[...] = jnp.where(lane < NA_HEAD_DIM, outs[0], outs[1]).astype(o_ref.dtype)


def context_attention(qkv, *, batch, seq, ctx_len):
    hp = NA_HEADS // 2
    ctx_blk0 = (batch * seq) // ctx_len
    return pl.pallas_call(
        _ctx_attn_body, grid=(batch, hp),
        in_specs=[pl.BlockSpec((ctx_len, LANES), lambda b, h: (ctx_blk0 + b, h)),
                  pl.BlockSpec((ctx_len, LANES), lambda b, h: (ctx_blk0 + b, hp + h)),
                  pl.BlockSpec((ctx_len, LANES), lambda b, h: (ctx_blk0 + b, 2 * hp + h))],
        out_specs=pl.BlockSpec((ctx_len, LANES), lambda b, h: (b, h)),
        out_shape=jax.ShapeDtypeStruct((batch * ctx_len, NA_WIDTH), BF16),
        compiler_params=_cparams(("parallel", "parallel")), name="context_attention",
    )(qkv, qkv, qkv)


FFT_UNROLL = 4


def _dft_cs(n):
    k = np.arange(n)
    ang = 2.0 * np.pi * ((k[:, None] * k[None, :]) % n) / n
    return np.cos(ang), np.sin(ang)


def _fourier_long_body(f_ref, wch_ref, ma_ref, mb_ref, o_ref, pr_ref, pi_ref, z_ref, *, n1, n2):
    c = FNET_GROUP_DIM
    p = jnp.dot(f_ref[...].astype(BF16), wch_ref[...], preferred_element_type=F32)
    pr_ref[...] = p[:, :c]
    pi_ref[...] = p[:, c:]

    def stage_a(jj, carry):
        for t in range(FFT_UNROLL):
            j = jj * FFT_UNROLL + t
            rows = pl.ds(j, n1, stride=n2)
            x = jnp.concatenate([pr_ref[rows, :], pi_ref[rows, :]], axis=1).astype(BF16)
            y = jnp.dot(ma_ref[j], x, preferred_element_type=F32)
            zr = y[:n1, :c] - y[n1:, c:]
            zi = y[n1:, :c] + y[:n1, c:]
            z_ref[pl.ds(j, n1, stride=2 * n2), :] = zr
            z_ref[pl.ds(n2 + j, n1, stride=2 * n2), :] = zi
        return carry

    lax.fori_loop(0, n2 // FFT_UNROLL, stage_a, 0)

    def stage_b(kk, carry):
        for t in range(FFT_UNROLL):
            k1 = kk * FFT_UNROLL + t
            z = z_ref[pl.ds(pl.multiple_of(k1 * 2 * n2, 2 * n2), 2 * n2), :].astype(BF16)
            y = jnp.dot(mb_ref[...], z, preferred_element_type=F32)
            o_ref[pl.ds(k1, n2, stride=n1), :] = y.astype(o_ref.dtype)
        return carry

    lax.fori_loop(0, n1 // FFT_UNROLL, stage_b, 0)


def fourier_long(f, *, batch, seq):
    c = FNET_GROUP_DIM
    groups = f.shape[1] // c
    n2 = 128
    n1 = seq // n2
    norm = 1.0 / math.sqrt(seq * c)
    cc, sc = _dft_cs(c)
    wch = np.concatenate([cc, -sc], axis=1)
    c1, s1 = _dft_cs(n1)
    j = np.arange(n2)
    k1 = np.arange(n1)
    tw = 2.0 * np.pi * (j[:, None] * k1[None, :]) / seq
    ar = np.cos(tw)[:, :, None] * c1[None] - np.sin(tw)[:, :, None] * s1[None]
    ai = -(np.cos(tw)[:, :, None] * s1[None] + np.sin(tw)[:, :, None] * c1[None])
    ma = np.concatenate([ar, ai], axis=1)
    c2, s2 = _dft_cs(n2)
    mb = np.concatenate([c2, s2], axis=1) * norm
    body = functools.partial(_fourier_long_body, n1=n1, n2=n2)
    return pl.pallas_call(
        body, grid=(batch, groups),
        in_specs=[pl.BlockSpec((seq, c), lambda b, g: (b, g)),
                  pl.BlockSpec((c, 2 * c), lambda b, g: (0, 0)),
                  pl.BlockSpec((n2, 2 * n1, n1), lambda b, g: (0, 0, 0)),
                  pl.BlockSpec((n2, 2 * n2), lambda b, g: (0, 0))],
        out_specs=pl.BlockSpec((seq, c), lambda b, g: (b, g)),
        out_shape=jax.ShapeDtypeStruct((batch * seq, groups * c), F32),
        scratch_shapes=[pltpu.VMEM((seq, c), F32), pltpu.VMEM((seq, c), F32), pltpu.VMEM((2 * seq, c), F32)],
        compiler_params=_cparams(("parallel", "parallel")), name="fourier_long",
    )(f, jnp.asarray(wch, BF16), jnp.asarray(ma, BF16), jnp.asarray(mb, BF16))


def _fourier_short_body(f_ref, wch_ref, mp_ref, o_ref):
    c = FNET_GROUP_DIM
    p = jnp.dot(f_ref[...].astype(BF16), wch_ref[...], preferred_element_type=F32)
    z = jnp.concatenate([p[:, :c], p[:, c:]], axis=0).astype(BF16)
    o_ref[...] = jnp.dot(mp_ref[...], z, preferred_element_type=F32).astype(o_ref.dtype)


def fourier_short(f, *, batch, n, row_block0):
    c = FNET_GROUP_DIM
    groups = f.shape[1] // c
    norm = 1.0 / math.sqrt(n * c)
    cc, sc = _dft_cs(c)
    wch = np.concatenate([cc, -sc], axis=1)
    cn, sn = _dft_cs(n)
    mp = np.concatenate([cn, sn], axis=1) * norm
    return pl.pallas_call(
        _fourier_short_body, grid=(batch, groups),
        in_specs=[pl.BlockSpec((n, c), lambda b, g: (row_block0 + b, g)),
                  pl.BlockSpec((c, 2 * c), lambda b, g: (0, 0)),
                  pl.BlockSpec((n, 2 * n), lambda b, g: (0, 0))],
        out_specs=pl.BlockSpec((n, c), lambda b, g: (b, g)),
        out_shape=jax.ShapeDtypeStruct((batch * n, groups * c), F32),
        compiler_params=_cparams(("parallel", "parallel")), name="fourier_short",
    )(f, jnp.asarray(wch, BF16), jnp.asarray(mp, BF16))


LOG2E = 1.4426950408889634


ONES_ROWS = 16


def _diff_body(lam_ref, q_ref, kl_ref, kn_ref, vtl_ref, kc_ref, vtc_ref, g_ref, o_ref,
               qs_ref, m_ref, acc_ref, st0_ref, *, nk, out_scale):
    j = pl.program_id(3)
    d = DIFF_HEAD_DIM

    def scores(k, comp):
        return lax.dot_general(k, qs_ref[comp], (((1,), (1,)), ((), ())),
                               preferred_element_type=F32)

    @pl.when(j == 0)
    def _():
        q = q_ref[...].astype(F32) * (d ** -0.5 * LOG2E)
        lane = lax.broadcasted_iota(jnp.int32, q.shape, 1)
        qs_ref[0] = jnp.where(lane < d, q, 0.0).astype(BF16)
        qs_ref[1] = jnp.where(lane >= d, q, 0.0).astype(BF16)
        m_ref[...] = jnp.full(m_ref.shape, NEG, F32)
        acc_ref[...] = jnp.zeros(acc_ref.shape, F32)
        st0_ref[...] = scores(kl_ref[...], 0)

    def softmax_pv(comp, st, vt):
        m_old = m_ref[comp]
        m_new = jnp.maximum(m_old, jnp.max(st, axis=0, keepdims=True))
        alpha = jnp.exp2(m_old - m_new)
        p = jnp.exp2((st - m_new).astype(BF16))
        acc_ref[comp] = alpha * acc_ref[comp] + jnp.dot(vt, p, preferred_element_type=F32)
        m_ref[comp] = m_new

    def step(k, vt, k_next):
        st1 = scores(k, 1)
        softmax_pv(0, st0_ref[:k.shape[0], :], vt)
        if k_next is not None:
            st0_ref[:k_next.shape[0], :] = scores(k_next, 0)
        softmax_pv(1, st1, vt)

    @pl.when(j < nk - 1)
    def _():
        step(kl_ref[...], vtl_ref[...], kn_ref[...])

    @pl.when(j == nk - 1)
    def _():
        step(kl_ref[...], vtl_ref[...], kc_ref[...])

    @pl.when(j == nk)
    def _():
        step(kc_ref[...], vtc_ref[...], None)
        lam = lam_ref[0]
        n = 2 * d
        o0 = acc_ref[0, :n, :] / acc_ref[0, n:n + 1, :]
        o1 = acc_ref[1, :n, :] / acc_ref[1, n:n + 1, :]
        ot = o0 - lam * o1
        yt = ot * lax.rsqrt(jnp.mean(ot * ot, axis=0, keepdims=True) + EPS)
        o_ref[...] = (yt.T * g_ref[...] * out_scale).astype(o_ref.dtype)


def diff_attention(qkv, vt, lam, subln_g, lambda_init, *, batch, seq, ctx_len, tq=1024, tk=2048):
    h = DIFF_HEADS
    nq = seq // tq
    nk = seq // tk
    vrows = 2 * DIFF_HEAD_DIM + ONES_ROWS
    ctx_blk0 = (batch * seq) // ctx_len
    body = functools.partial(_diff_body, nk=nk, out_scale=1.0 - lambda_init)
    assert nk >= 2 and ctx_len <= tk
    lat = lambda b, hh, i, j: b * nk + jnp.minimum(j, nk - 1)
    nxt = lambda b, hh, i, j: b * nk + jnp.minimum(j + 1, nk - 1)
    return pl.pallas_call(
        body, grid=(batch, h, nq, nk + 1),
        in_specs=[pl.BlockSpec(memory_space=pltpu.SMEM),
                  pl.BlockSpec((tq, LANES), lambda b, hh, i, j: (b * nq + i, hh)),
                  pl.BlockSpec((tk, LANES), lambda b, hh, i, j: (lat(b, hh, i, j), h + hh)),
                  pl.BlockSpec((tk, LANES), lambda b, hh, i, j: (nxt(b, hh, i, j), h + hh)),
                  pl.BlockSpec((vrows, tk), lambda b, hh, i, j: (hh, lat(b, hh, i, j))),
                  pl.BlockSpec((ctx_len, LANES), lambda b, hh, i, j: (ctx_blk0 + b, h + hh)),
                  pl.BlockSpec((vrows, ctx_len), lambda b, hh, i, j: (hh, ctx_blk0 + b)),
                  pl.BlockSpec((1, LANES), lambda b, hh, i, j: (0, 0))],
        out_specs=pl.BlockSpec((tq, LANES), lambda b, hh, i, j: (b * nq + i, hh)),
        out_shape=jax.ShapeDtypeStruct((batch * seq, h * LANES), BF16),
        scratch_shapes=[pltpu.VMEM((2, tq, LANES), BF16), pltpu.VMEM((2, 1, tq), F32),
                        pltpu.VMEM((2, vrows, tq), F32), pltpu.VMEM((tk, tq), F32)],
        compiler_params=_cparams(("parallel", "parallel", "parallel", "arbitrary")),
        name="diff_attention",
    )(lam, qkv, qkv, qkv, vt, qkv, vt, subln_g)


def _with_ones_rows(vt, heads):
    hw, rows = vt.shape
    blocks = vt.reshape(heads, hw // heads, rows)
    ones = jnp.ones((heads, ONES_ROWS, rows), vt.dtype)
    return jnp.concatenate([blocks, ones], axis=1).reshape(hw + heads * ONES_ROWS, rows)


def _extract16(s, pos):
    big = float(2 ** 20)
    rank = jnp.full(s.shape, float(PEER_TOPK), F32)
    vals = []
    for r in range(PEER_TOPK):
        m = jnp.max(s, axis=0, keepdims=True)
        sel = jnp.min(jnp.where(s == m, pos, big), axis=0, keepdims=True)
        hit = pos == sel
        vals.append(m)
        rank = jnp.where(hit, float(r), rank)
        s = jnp.where(hit, -jnp.inf, s)
    return vals, rank, rank < float(PEER_TOPK)


_CAND_GROUPS = ((0, 16), (1, 8), (2, 5), (3, 4), (4, 3), (5, 2), (6, 2), (7, 2))


def _peer_topk_body(q_ref, keys_ref, r1_ref, e1_ref, n0_ref, c0_ref):
    q = q_ref[...]
    kd = q.shape[1] // 2
    tb = q.shape[0]
    res = []
    for half in range(2):
        s = lax.dot_general(keys_ref[0, half], q[:, half * kd:(half + 1) * kd],
                            (((1,), (1,)), ((), ())), preferred_element_type=F32)
        pos = lax.broadcasted_iota(jnp.int32, s.shape, 0).astype(F32)
        vals, rank, _ = _extract16(s, pos)
        res.append((s, vals, rank))
    (s0, v0, rank0), (s1, v1, rank1) = res
    v1_lo = jnp.concatenate(v1[:SUBLANES], axis=0)
    v1_hi = jnp.concatenate(v1[SUBLANES:], axis=0)
    row8 = lax.broadcasted_iota(jnp.int32, (SUBLANES, tb), 0)
    pieces, poses = [], []
    for a, nb in _CAND_GROUPS:
        if nb > SUBLANES:
            pieces += [v0[a] + v1_lo, v0[a] + v1_hi]
            poses += [a * PEER_TOPK + row8, a * PEER_TOPK + SUBLANES + row8]
        else:
            pieces.append(jnp.where(row8 < nb, v0[a] + v1_lo, -jnp.inf))
            poses.append(a * PEER_TOPK + row8)
    pieces.append(jnp.concatenate(v0[SUBLANES:], axis=0) + v1[0])
    poses.append((row8 + SUBLANES) * PEER_TOPK)
    cand = jnp.concatenate(pieces, axis=0)
    cpos = jnp.concatenate(poses, axis=0).astype(F32)
    best, _, taken = _extract16(cand, cpos)
    z = sum(jnp.exp(b - best[0]) for b in best)
    takenf = taken.astype(F32)
    counts = []
    row = 0
    for a, nb in _CAND_GROUPS:
        rows = 2 * SUBLANES if nb > SUBLANES else SUBLANES
        counts.append(jnp.sum(takenf[row:row + rows], axis=0, keepdims=True))
        row += rows
    counts += [takenf[row + i:row + i + 1] for i in range(SUBLANES)]
    n0 = jnp.zeros(s0.shape, F32)
    for a in range(PEER_TOPK):
        n0 = jnp.where(rank0 == float(a), counts[a], n0)
    r1_ref[0] = rank1.astype(r1_ref.dtype)
    e1_ref[0] = jnp.exp(s1 - v1[0]).astype(e1_ref.dtype)
    n0_ref[0] = n0
    c0_ref[0] = jnp.exp(s0 - v0[0]) / z


def peer_topk(q, sub_keys, *, tb=1024):
    t = q.shape[0]
    h = PEER_HEADS
    spec = pl.BlockSpec((1, PEER_N_KEYS, tb), lambda i, hh: (hh, 0, i))
    shapes = [jax.ShapeDtypeStruct((h, PEER_N_KEYS, t), dt) for dt in (BF16, BF16, F32, F32)]
    return pl.pallas_call(
        _peer_topk_body, grid=(t // tb, h),
        in_specs=[pl.BlockSpec((tb, 2 * LANES), lambda i, hh: (i, hh)),
                  pl.BlockSpec((1, 2, PEER_N_KEYS, LANES), lambda i, hh: (hh, 0, 0, 0))],
        out_specs=[spec] * 4, out_shape=shapes,
        compiler_params=_cparams(("parallel", "parallel")), name="peer_topk",
    )(q, sub_keys)


PEER_PIPE_LAG = 2
GELU_C1 = math.sqrt(2.0 / math.pi)
GELU_C2 = 0.044715 * GELU_C1


def _peer_mlp_body(ht_ref, u_ref, vt_ref, r1_ref, e1_ref, n0_ref, c0_ref, x_ref, gate_ref, o_ref,
                   at_ref, wt_ref, acc_ref, *, blocks_per_step, n_chunks, n_flat):
    s = pl.program_id(0)
    cur = s % 2
    prev = 1 - cur
    c1 = jnp.clip(s - 1, 0, n_flat - 1) % n_chunks
    c2 = jnp.clip(s - 2, 0, n_flat - 1) % n_chunks

    @pl.when(s == 0)
    def _():
        at_ref[...] = jnp.zeros(at_ref.shape, F32)
        wt_ref[...] = jnp.zeros(wt_ref.shape, BF16)

    @pl.when(c2 == 0)
    def _():
        acc_ref[...] = jnp.zeros(acc_ref.shape, F32)

    acc_ref[...] += jnp.dot(vt_ref[...], wt_ref[prev], preferred_element_type=F32)

    tiles = []
    for ib in range(blocks_per_step):
        i = c1 * blocks_per_step + ib
        g = None
        for h in range(PEER_HEADS):
            n0 = n0_ref[h, pl.ds(i, 1), :].astype(BF16)
            c0 = c0_ref[h, pl.ds(i, 1), :].astype(BF16)
            term = jnp.where(r1_ref[h] < n0, e1_ref[h], jnp.zeros((), BF16)) * c0
            g = term if g is None else g + term
        tiles.append(g)
    gt = jnp.concatenate(tiles, axis=0)
    a = at_ref[prev].astype(BF16)
    inner = a * (GELU_C1 + GELU_C2 * (a * a))
    half = 0.5 * a
    wt_ref[cur] = (half + half * jnp.tanh(inner)) * gt

    at_ref[cur] = jnp.dot(u_ref[...], ht_ref[...], preferred_element_type=F32)

    @pl.when((c2 == n_chunks - 1) & (s >= PEER_PIPE_LAG))
    def _():
        o_ref[...] = x_ref[...] + gate_ref[0] * acc_ref[...].T


def peer_experts(x, ht, routing, u, vt, g_f, *, tm=512, te=1024, seg_rows=8192):
    t, d = x.shape
    n_exp = u.shape[0]
    n_seg = g_f.shape[0]
    n_chunks = n_exp // te
    n_flat = (t // tm) * n_chunks
    seg = functools.partial(_seg_of_block, tm=tm, seg_rows=seg_rows, n_seg=n_seg)

    def lagged(lag):
        f = lambda s: jnp.clip(s - lag, 0, n_flat - 1)
        return (lambda s: f(s) // n_chunks), (lambda s: f(s) % n_chunks)

    (blk0, chunk0), (blk1, _), (blk2, chunk2) = lagged(0), lagged(1), lagged(2)
    route_spec = pl.BlockSpec((PEER_HEADS, PEER_N_KEYS, tm), lambda s: (0, 0, blk1(s)))
    body = functools.partial(_peer_mlp_body, blocks_per_step=te // PEER_N_KEYS, n_chunks=n_chunks, n_flat=n_flat)
    return pl.pallas_call(
        body, grid=(n_flat + PEER_PIPE_LAG,),
        in_specs=[pl.BlockSpec((d, tm), lambda s: (0, blk0(s))),
                  pl.BlockSpec((te, d), lambda s: (chunk0(s), 0)),
                  pl.BlockSpec((d, te), lambda s: (0, chunk2(s))),
                  route_spec, route_spec, route_spec, route_spec,
                  pl.BlockSpec((tm, d), lambda s: (blk2(s), 0)),
                  pl.BlockSpec((1, 1, d), lambda s: (seg(blk2(s)), 0, 0))],
        out_specs=pl.BlockSpec((tm, d), lambda s: (blk2(s), 0)),
        out_shape=jax.ShapeDtypeStruct((t, d), F32),
        scratch_shapes=[pltpu.VMEM((2, te, tm), F32), pltpu.VMEM((2, te, tm), BF16), pltpu.VMEM((d, tm), F32)],
        compiler_params=_cparams(("arbitrary",)), name="peer_experts",
    )(ht, u, vt, *routing, x, g_f)


def _rms_body(x_ref, g_ref, o_ref):
    xf = x_ref[...]
    o_ref[...] = xf * lax.rsqrt(jnp.mean(xf * xf, axis=-1, keepdims=True) + EPS) * g_ref[...]


def final_rms_norm(x, g, *, tm=1024):
    m, d = x.shape
    return pl.pallas_call(
        _rms_body, grid=(m // tm,),
        in_specs=[pl.BlockSpec((tm, d), lambda i: (i, 0)), pl.BlockSpec((1, d), lambda i: (0, 0))],
        out_specs=pl.BlockSpec((tm, d), lambda i: (i, 0)),
        out_shape=jax.ShapeDtypeStruct((m, d), F32),
        compiler_params=_cparams(("parallel",)), name="final_rms_norm")(x, g)


def _rope_tables(seq, extra_rows):
    quarter = DIFF_HEAD_DIM // 4
    t = jnp.arange(seq)
    row = (t // GRID_W).astype(F32)
    col = (t % GRID_W).astype(F32)
    inv = ROPE_BASE ** (-jnp.arange(quarter, dtype=F32) / quarter)
    cos_r, sin_r = jnp.cos(row[:, None] * inv), jnp.sin(row[:, None] * inv)
    cos_c, sin_c = jnp.cos(col[:, None] * inv), jnp.sin(col[:, None] * inv)
    cos64 = jnp.concatenate([cos_r, cos_r, cos_c, cos_c], axis=1)
    sin64 = jnp.concatenate([-sin_r, sin_r, -sin_c, sin_c], axis=1)
    cos = jnp.concatenate([cos64, cos64], axis=1)
    sin = jnp.concatenate([sin64, sin64], axis=1)
    cos = jnp.concatenate([cos, jnp.ones((extra_rows, LANES), F32)], axis=0)
    sin = jnp.concatenate([sin, jnp.zeros((extra_rows, LANES), F32)], axis=0)
    return cos, sin


def kernel(x, c, ctx, c_ctx, w_mod, b_mod, norm_mix_g, norm_ffn_g, even_w_in, even_w_out, na_rpb,
           odd_w_in, odd_w_out, diff_lambda_q1, diff_lambda_k1, diff_lambda_q2, diff_lambda_k2,
           diff_subln_g, peer_w_q, peer_sub_keys, peer_u, peer_v, final_norm_g):
    batch, seq, d = x.shape
    ctx_len = ctx.shape[1]
    depth = w_mod.shape[0]
    n_lat = batch * seq
    tm = 512
    xs = jnp.concatenate([x.reshape(n_lat, d), ctx.reshape(batch * ctx_len, d)], axis=0)
    cond = jnp.concatenate([c, c_ctx[None], jnp.zeros((SUBLANES - batch - 1, d), F32)], axis=0)
    cond = jax.nn.silu(cond)
    n_seg = batch + 1

    for layer in range(depth):
        last = layer == depth - 1
        jl = layer // 2
        mod = fused_matmul(cond, w_mod[layer].astype(BF16), tm=SUBLANES, tn=1024)[:n_seg] + b_mod[layer]
        sh_m, sc_m, g_m, sh_f, sc_f, g_f = [m_[:, None, :] for m_ in jnp.split(mod, 6, axis=-1)]
        norm_mix = (norm_mix_g[layer][None], sh_m, sc_m)
        norm_ffn = (norm_ffn_g[layer][None], sh_f, sc_f)
        if layer % 2 == 0:
            w_in = even_w_in[jl].astype(BF16)
            w_out = even_w_out[jl].astype(BF16)
            qkv = fused_matmul(xs, w_in[:, :3 * NA_WIDTH], norm=norm_mix, out_dtype=BF16, tm=tm)
            f = fused_matmul(xs, w_in[:, 3 * NA_WIDTH:], norm=norm_mix, tm=tm)
            attn = neighbourhood_attention(qkv, na_rpb[jl], batch=batch, seq=seq, ctx_len=ctx_len)
            four = fourier_long(f, batch=batch, seq=seq)
            if not last:
                attn = jnp.concatenate([attn, context_attention(qkv, batch=batch, seq=seq, ctx_len=ctx_len)], axis=0)
                four = jnp.concatenate([four, fourier_short(f, batch=batch, n=ctx_len, row_block0=n_lat // ctx_len)], axis=0)
            rows = attn.shape[0]
            xs_mix = fused_matmul(attn, w_out[:NA_WIDTH], res=(xs[:rows], g_m), tm=tm)
            xs_mix = fused_matmul(four, w_out[NA_WIDTH:], res=(xs_mix, g_m), tm=tm)
        else:
            lambda_init = 0.8 - 0.6 * math.exp(-0.3 * layer)
            lam = (jnp.exp(jnp.sum(diff_lambda_q1[jl] * diff_lambda_k1[jl]))
                   - jnp.exp(jnp.sum(diff_lambda_q2[jl] * diff_lambda_k2[jl])) + lambda_init).reshape(1)
            cos, sin = _rope_tables(seq, tm)
            v_width = DIFF_HEADS * 2 * DIFF_HEAD_DIM
            qkv, vt = fused_matmul(xs, odd_w_in[jl].astype(BF16), norm=norm_mix, out_dtype=BF16, tm=tm,
                                   rope=(cos, sin, 2 * DIFF_HEADS * 2 * DIFF_HEAD_DIM, seq // tm, n_lat // tm),
                                   t_cols=v_width)
            o = diff_attention(qkv, _with_ones_rows(vt, DIFF_HEADS), lam, diff_subln_g[jl][None], lambda_init,
                               batch=batch, seq=seq, ctx_len=ctx_len)
            if not last:
                raise NotImplementedError("context output of a differential-attention layer")
            xs_mix = fused_matmul(o, odd_w_out[jl].astype(BF16), res=(xs, g_m), tm=tm)
        q, ht = fused_matmul(xs_mix, peer_w_q[layer].astype(BF16), norm=norm_ffn, emit_ht=True, tm=tm)
        routing = peer_topk(q, peer_sub_keys[layer])
        xs = peer_experts(xs_mix, ht, routing, peer_u[layer].astype(BF16), peer_v[layer].astype(BF16).T, g_f)
        if not last and xs.shape[0] == n_lat:
            raise NotImplementedError("context stream dropped before the last layer")
    out = final_rms_norm(xs[:n_lat], final_norm_g[None])
    return out.reshape(batch, seq, d)
```

```python
import functools
import math

import jax
import jax.numpy as jnp
import numpy as np
from jax import lax
from jax.experimental import pallas as pl
from jax.experimental.pallas import tpu as pltpu

F32 = jnp.float32
BF16 = jnp.bfloat16

D_MODEL = 1024
GRID_W = 64
EPS = 1e-6
ROPE_BASE = 10000.0
NA_HEADS = 8
NA_HEAD_DIM = 64
NA_WIDTH = 512
NA_WIN_H = 8
NA_WIN_W = 16
FNET_GROUP_DIM = 128
DIFF_HEADS = 8
DIFF_HEAD_DIM = 64
PEER_HEADS = 8
PEER_N_KEYS = 128
PEER_TOPK = 16
LANES = 128
SUBLANES = 8
NEG = -1e30
VMEM_LIMIT = 56 * 1024 * 1024


def _cparams(sem, vmem=VMEM_LIMIT):
    return pltpu.CompilerParams(dimension_semantics=sem, vmem_limit_bytes=vmem)


def _mm_body(*refs, has_norm, emit_ht, rope_tiles, has_res, t_from):
    it = iter(refs)
    x_ref = next(it)
    if has_norm:
        g_ref, sh_ref, sc_ref = next(it), next(it), next(it)
    w_ref = next(it)
    if rope_tiles:
        cos_ref, sin_ref = next(it), next(it)
    if has_res:
        res_ref, gate_ref = next(it), next(it)
    o_ref = next(it)
    if emit_ht:
        ht_ref = next(it)
    if t_from is not None:
        ot_ref = next(it)
    if has_norm:
        xs_ref = next(it)
    j = pl.program_id(1)

    if has_norm:
        @pl.when(j == 0)
        def _():
            xf = x_ref[...]
            y = xf * lax.rsqrt(jnp.mean(xf * xf, axis=-1, keepdims=True) + EPS)
            y = y * g_ref[...] * (1.0 + sc_ref[0]) + sh_ref[0]
            xs_ref[...] = y.astype(BF16)
            if emit_ht:
                ht_ref[...] = y.T.astype(BF16)
        a = xs_ref[...]
    else:
        a = x_ref[...].astype(BF16)
    acc = jnp.dot(a, w_ref[...], preferred_element_type=F32)

    if t_from is not None:
        @pl.when(j >= t_from)
        def _():
            ot_ref[...] = acc.T.astype(ot_ref.dtype)

    def finish(v):
        if has_res:
            v = res_ref[...] + gate_ref[0] * v
        o_ref[...] = v.astype(o_ref.dtype)

    if rope_tiles:
        @pl.when(j < rope_tiles)
        def _():
            tn = acc.shape[1]
            reps = tn // LANES
            c = jnp.concatenate([cos_ref[...]] * reps, axis=1)
            s = jnp.concatenate([sin_ref[...]] * reps, axis=1)
            lane = lax.broadcasted_iota(jnp.int32, acc.shape, 1)
            first = (lane % 32) < 16
            partner = jnp.where(first, pltpu.roll(acc, tn - 16, 1), pltpu.roll(acc, 16, 1))
            finish(acc * c + partner * s)

        @pl.when(j >= rope_tiles)
        def _():
            finish(acc)
    else:
        finish(acc)


def _seg_of_block(i, tm, seg_rows, n_seg):
    return jnp.minimum((i * tm) // seg_rows, n_seg - 1)


def fused_matmul(x, w, *, norm=None, rope=None, res=None, emit_ht=False, t_cols=0, out_dtype=F32,
                 tm=512, tn=1024, seg_rows=8192):
    m, k = x.shape
    n = w.shape[1]
    tm = min(tm, m)
    tn = max(t for t in range(LANES, min(tn, n) + 1, LANES) if n % t == 0)
    assert m % tm == 0 and n % tn == 0
    grid = (m // tm, n // tn)
    in_specs = [pl.BlockSpec((tm, k), lambda i, j: (i, 0))]
    args = [x]
    n_seg = 1
    if norm is not None:
        g, sh, sc = norm
        n_seg = sh.shape[0]
        seg = functools.partial(_seg_of_block, tm=tm, seg_rows=seg_rows, n_seg=n_seg)
        in_specs += [pl.BlockSpec((1, k), lambda i, j: (0, 0)),
                     pl.BlockSpec((1, 1, k), lambda i, j: (seg(i), 0, 0)),
                     pl.BlockSpec((1, 1, k), lambda i, j: (seg(i), 0, 0))]
        args += [g, sh, sc]
    in_specs.append(pl.BlockSpec((k, tn), lambda i, j: (0, j)))
    args.append(w)
    rope_tiles = 0
    if rope is not None:
        cos, sin, n_rope_cols, n_pos_blocks, n_lat_blocks = rope
        rope_tiles = n_rope_cols // tn
        pos = lambda i, j: (jnp.where(i < n_lat_blocks, i % n_pos_blocks, n_pos_blocks), 0)
        in_specs += [pl.BlockSpec((tm, LANES), pos), pl.BlockSpec((tm, LANES), pos)]
        args += [cos, sin]
    if res is not None:
        r, gate = res
        n_seg_r = gate.shape[0]
        segr = functools.partial(_seg_of_block, tm=tm, seg_rows=seg_rows, n_seg=n_seg_r)
        in_specs += [pl.BlockSpec((tm, tn), lambda i, j: (i, j)),
                     pl.BlockSpec((1, 1, tn), lambda i, j: (segr(i), 0, j))]
        args += [r, gate]
    out_shape = [jax.ShapeDtypeStruct((m, n), out_dtype)]
    out_specs = [pl.BlockSpec((tm, tn), lambda i, j: (i, j))]
    if emit_ht:
        out_shape.append(jax.ShapeDtypeStruct((k, m), BF16))
        out_specs.append(pl.BlockSpec((k, tm), lambda i, j: (0, i)))
    t_from = None
    if t_cols:
        assert t_cols % tn == 0
        t_from = (n - t_cols) // tn
        out_shape.append(jax.ShapeDtypeStruct((t_cols, m), BF16))
        out_specs.append(pl.BlockSpec((tn, tm), lambda i, j: (jnp.maximum(j - t_from, 0), i)))
    scratch = [pltpu.VMEM((tm, k), BF16)] if norm is not None else []
    body = functools.partial(_mm_body, has_norm=norm is not None, emit_ht=emit_ht,
                             rope_tiles=rope_tiles, has_res=res is not None, t_from=t_from)
    outs = pl.pallas_call(
        body, grid=grid, in_specs=in_specs, out_specs=out_specs, out_shape=out_shape,
        scratch_shapes=scratch, compiler_params=_cparams(("parallel", "arbitrary")),
        name="fused_matmul")(*args)
    return outs if len(outs) > 1 else outs[0]


NA_GROUP = 4
NA_UNION = NA_WIN_H + NA_GROUP


def _na_body(q_ref, k_ref, v_ref, kc_ref, vc_ref, bias_ref, o_ref, *, rows):
    nq = NA_GROUP * GRID_W
    nkeys = NA_UNION * GRID_W
    lane = lax.broadcasted_iota(jnp.int32, (nq, LANES), 1)
    head_masks = [lane < NA_HEAD_DIM, lane >= NA_HEAD_DIM]
    kc = kc_ref[...]
    vc = vc_ref[...]
    scale = NA_HEAD_DIM ** -0.5
    nt = (((1,), (1,)), ((), ()))
    n_groups = rows // NA_GROUP

    def group_body(g, carry):
        r0 = g * NA_GROUP
        u0 = jnp.clip(r0 - NA_WIN_H // 2, 0, rows - NA_UNION)
        pattern = jnp.where(g == 0, 0, jnp.where(g == n_groups - 1, 2, 1))
        q_rows = pl.ds(pl.multiple_of(r0 * GRID_W, nq), nq)
        q = q_ref[q_rows, :]
        k_start = pl.multiple_of(u0 * GRID_W, NA_GROUP * GRID_W)
        kw = k_ref[pl.ds(k_start, nkeys), :]
        vw = v_ref[pl.ds(k_start, nkeys), :]
        outs = []
        for hh in range(2):
            qm = jnp.where(head_masks[hh], q, jnp.zeros_like(q))
            bias = bias_ref[pattern, :, hh].reshape(nq, nkeys)
            s_nb = lax.dot_general(qm, kw, nt, preferred_element_type=F32) * scale + bias
            s_cx = lax.dot_general(qm, kc, nt, preferred_element_type=F32) * scale
            m = jnp.maximum(jnp.max(s_nb, axis=1, keepdims=True), jnp.max(s_cx, axis=1, keepdims=True))
            p_nb = jnp.exp(s_nb - m)
            p_cx = jnp.exp(s_cx - m)
            l = jnp.sum(p_nb, axis=1, keepdims=True) + jnp.sum(p_cx, axis=1, keepdims=True)
            o = (jnp.dot(p_nb.astype(BF16), vw, preferred_element_type=F32)
                 + jnp.dot(p_cx.astype(BF16), vc, preferred_element_type=F32))
            outs.append(o / l)
        o_ref[q_rows, :] = jnp.where(head_masks[0], outs[0], outs[1]).astype(o_ref.dtype)
        return carry

    lax.fori_loop(0, n_groups, group_body, 0)


def _na_bias_table(rpb):
    col = np.arange(GRID_W)
    col_start = np.clip(col - NA_WIN_W // 2, 0, GRID_W - NA_WIN_W)
    kc = np.arange(GRID_W)
    inside = (kc[None, :] >= col_start[:, None]) & (kc[None, :] < col_start[:, None] + NA_WIN_W)
    dc = np.clip(kc[None, :] - col[:, None] + NA_WIN_W - 1, 0, 2 * NA_WIN_W - 2)
    per_dr = jnp.where(inside[None, None], rpb[:, :, dc], NEG)
    masked = jnp.full((NA_HEADS, GRID_W, GRID_W), NEG, F32)
    half = NA_WIN_H // 2
    patterns = [(0, lambda dr: -dr), (-half, lambda dr: -half), (-NA_WIN_H, lambda dr: -half - dr)]
    tabs = []
    for union_off, win_start in patterns:
        per_row = []
        for dr in range(NA_GROUP):
            blocks = []
            for kk in range(NA_UNION):
                delta = union_off + kk - dr
                ok = win_start(dr) <= delta < win_start(dr) + NA_WIN_H
                blocks.append(per_dr[:, delta + NA_WIN_H - 1] if ok else masked)
            per_row.append(jnp.concatenate(blocks, axis=-1))
        tabs.append(jnp.stack(per_row))
    return jnp.stack(tabs)


def neighbourhood_attention(qkv, rpb, *, batch, seq, ctx_len):
    rows = seq // GRID_W
    assert rows % NA_GROUP == 0 and rows >= NA_UNION + NA_GROUP
    bias = _na_bias_table(rpb)
    hp = NA_HEADS // 2
    ctx_blk0 = (batch * seq) // ctx_len
    body = functools.partial(_na_body, rows=rows)
    return pl.pallas_call(
        body, grid=(batch, hp),
        in_specs=[pl.BlockSpec((seq, LANES), lambda b, h: (b, h)),
                  pl.BlockSpec((seq, LANES), lambda b, h: (b, hp + h)),
                  pl.BlockSpec((seq, LANES), lambda b, h: (b, 2 * hp + h)),
                  pl.BlockSpec((ctx_len, LANES), lambda b, h: (ctx_blk0 + b, hp + h)),
                  pl.BlockSpec((ctx_len, LANES), lambda b, h: (ctx_blk0 + b, 2 * hp + h)),
                  pl.BlockSpec((3, NA_GROUP, 2, GRID_W, NA_UNION * GRID_W), lambda b, h: (0, 0, h, 0, 0))],
        out_specs=pl.BlockSpec((seq, LANES), lambda b, h: (b, h)),
        out_shape=jax.ShapeDtypeStruct((batch * seq, NA_WIDTH), BF16),
        compiler_params=_cparams(("parallel", "parallel")), name="neighbourhood_attention",
    )(qkv, qkv, qkv, qkv, qkv, bias)


def _ctx_attn_body(q_ref, k_ref, v_ref, o_ref):
    lane = lax.broadcasted_iota(jnp.int32, q_ref.shape, 1)
    q = q_ref[...]
    k = k_ref[...]
    v = v_ref[...]
    scale = NA_HEAD_DIM ** -0.5
    outs = []
    for hh in range(2):
        msk = (lane < NA_HEAD_DIM) if hh == 0 else (lane >= NA_HEAD_DIM)
        qm = jnp.where(msk, q, jnp.zeros_like(q))
        s = lax.dot_general(qm, k, (((1,), (1,)), ((), ())), preferred_element_type=F32) * scale
        m = jnp.max(s, axis=1, keepdims=True)
        p = jnp.exp(s - m)
        l = jnp.sum(p, axis=1, keepdims=True)
        outs.append(jnp.dot(p.astype(BF16), v, preferred_element_type=F32) / l)
    o_ref[...] = jnp.where(lane < NA_HEAD_DIM, outs[0], outs[1]).astype(o_ref.dtype)


def context_attention(qkv, *, batch, seq, ctx_len):
    hp = NA_HEADS // 2
    ctx_blk0 = (batch * seq) // ctx_len
    return pl.pallas_call(
        _ctx_attn_body, grid=(batch, hp),
        in_specs=[pl.BlockSpec((ctx_len, LANES), lambda b, h: (ctx_blk0 + b, h)),
                  pl.BlockSpec((ctx_len, LANES), lambda b, h: (ctx_blk0 + b, hp + h)),
                  pl.BlockSpec((ctx_len, LANES), lambda b, h: (ctx_blk0 + b, 2 * hp + h))],
        out_specs=pl.BlockSpec((ctx_len, LANES), lambda b, h: (b, h)),
        out_shape=jax.ShapeDtypeStruct((batch * ctx_len, NA_WIDTH), BF16),
        compiler_params=_cparams(("parallel", "parallel")), name="context_attention",
    )(qkv, qkv, qkv)


FFT_UNROLL = 4


def _dft_cs(n):
    k = np.arange(n)
    ang = 2.0 * np.pi * ((k[:, None] * k[None, :]) % n) / n
    return np.cos(ang), np.sin(ang)


def _fourier_long_body(f_ref, wch_ref, ma_ref, mb_ref, o_ref, pr_ref, pi_ref, z_ref, *, n1, n2):
    c = FNET_GROUP_DIM
    p = jnp.dot(f_ref[...].astype(BF16), wch_ref[...], preferred_element_type=F32)
    pr_ref[...] = p[:, :c]
    pi_ref[...] = p[:, c:]

    def stage_a(jj, carry):
        for t in range(FFT_UNROLL):
            j = jj * FFT_UNROLL + t
            rows = pl.ds(j, n1, stride=n2)
            x = jnp.concatenate([pr_ref[rows, :], pi_ref[rows, :]], axis=1).astype(BF16)
            y = jnp.dot(ma_ref[j], x, preferred_element_type=F32)
            zr = y[:n1, :c] - y[n1:, c:]
            zi = y[n1:, :c] + y[:n1, c:]
            z_ref[pl.ds(j, n1, stride=2 * n2), :] = zr
            z_ref[pl.ds(n2 + j, n1, stride=2 * n2), :] = zi
        return carry

    lax.fori_loop(0, n2 // FFT_UNROLL, stage_a, 0)

    def stage_b(kk, carry):
        for t in range(FFT_UNROLL):
            k1 = kk * FFT_UNROLL + t
            z = z_ref[pl.ds(pl.multiple_of(k1 * 2 * n2, 2 * n2), 2 * n2), :].astype(BF16)
            y = jnp.dot(mb_ref[...], z, preferred_element_type=F32)
            o_ref[pl.ds(k1, n2, stride=n1), :] = y.astype(o_ref.dtype)
        return carry

    lax.fori_loop(0, n1 // FFT_UNROLL, stage_b, 0)


def fourier_long(f, *, batch, seq):
    c = FNET_GROUP_DIM
    groups = f.shape[1] // c
    n2 = 128
    n1 = seq // n2
    norm = 1.0 / math.sqrt(seq * c)
    cc, sc = _dft_cs(c)
    wch = np.concatenate([cc, -sc], axis=1)
    c1, s1 = _dft_cs(n1)
    j = np.arange(n2)
    k1 = np.arange(n1)
    tw = 2.0 * np.pi * (j[:, None] * k1[None, :]) / seq
    ar = np.cos(tw)[:, :, None] * c1[None] - np.sin(tw)[:, :, None] * s1[None]
    ai = -(np.cos(tw)[:, :, None] * s1[None] + np.sin(tw)[:, :, None] * c1[None])
    ma = np.concatenate([ar, ai], axis=1)
    c2, s2 = _dft_cs(n2)
    mb = np.concatenate([c2, s2], axis=1) * norm
    body = functools.partial(_fourier_long_body, n1=n1, n2=n2)
    return pl.pallas_call(
        body, grid=(batch, groups),
        in_specs=[pl.BlockSpec((seq, c), lambda b, g: (b, g)),
                  pl.BlockSpec((c, 2 * c), lambda b, g: (0, 0)),
                  pl.BlockSpec((n2, 2 * n1, n1), lambda b, g: (0, 0, 0)),
                  pl.BlockSpec((n2, 2 * n2), lambda b, g: (0, 0))],
        out_specs=pl.BlockSpec((seq, c), lambda b, g: (b, g)),
        out_shape=jax.ShapeDtypeStruct((batch * seq, groups * c), F32),
        scratch_shapes=[pltpu.VMEM((seq, c), F32), pltpu.VMEM((seq, c), F32), pltpu.VMEM((2 * seq, c), F32)],
        compiler_params=_cparams(("parallel", "parallel")), name="fourier_long",
    )(f, jnp.asarray(wch, BF16), jnp.asarray(ma, BF16), jnp.asarray(mb, BF16))


def _fourier_short_body(f_ref, wch_ref, mp_ref, o_ref):
    c = FNET_GROUP_DIM
    p = jnp.dot(f_ref[...].astype(BF16), wch_ref[...], preferred_element_type=F32)
    z = jnp.concatenate([p[:, :c], p[:, c:]], axis=0).astype(BF16)
    o_ref[...] = jnp.dot(mp_ref[...], z, preferred_element_type=F32).astype(o_ref.dtype)


def fourier_short(f, *, batch, n, row_block0):
    c = FNET_GROUP_DIM
    groups = f.shape[1] // c
    norm = 1.0 / math.sqrt(n * c)
    cc, sc = _dft_cs(c)
    wch = np.concatenate([cc, -sc], axis=1)
    cn, sn = _dft_cs(n)
    mp = np.concatenate([cn, sn], axis=1) * norm
    return pl.pallas_call(
        _fourier_short_body, grid=(batch, groups),
        in_specs=[pl.BlockSpec((n, c), lambda b, g: (row_block0 + b, g)),
                  pl.BlockSpec((c, 2 * c), lambda b, g: (0, 0)),
                  pl.BlockSpec((n, 2 * n), lambda b, g: (0, 0))],
        out_specs=pl.BlockSpec((n, c), lambda b, g: (b, g)),
        out_shape=jax.ShapeDtypeStruct((batch * n, groups * c), F32),
        compiler_params=_cparams(("parallel", "parallel")), name="fourier_short",
    )(f, jnp.asarray(wch, BF16), jnp.asarray(mp, BF16))


LOG2E = 1.4426950408889634


ONES_ROWS = 16


def _diff_body(lam_ref, q_ref, kl_ref, kn_ref, vtl_ref, kc_ref, vtc_ref, g_ref, o_ref,
               qs_ref, m_ref, acc_ref, st0_ref, *, nk, out_scale):
    j = pl.program_id(3)
    d = DIFF_HEAD_DIM

    def scores(k, comp):
        return lax.dot_general(k, qs_ref[comp], (((1,), (1,)), ((), ())),
                               preferred_element_type=F32)

    @pl.when(j == 0)
    def _():
        q = q_ref[...].astype(F32) * (d ** -0.5 * LOG2E)
        lane = lax.broadcasted_iota(jnp.int32, q.shape, 1)
        qs_ref[0] = jnp.where(lane < d, q, 0.0).astype(BF16)
        qs_ref[1] = jnp.where(lane >= d, q, 0.0).astype(BF16)
        m_ref[...] = jnp.full(m_ref.shape, NEG, F32)
        acc_ref[...] = jnp.zeros(acc_ref.shape, F32)
        st0_ref[...] = scores(kl_ref[...], 0)

    def softmax_pv(comp, st, vt):
        m_old = m_ref[comp]
        m_new = jnp.maximum(m_old, jnp.max(st, axis=0, keepdims=True))
        alpha = jnp.exp2(m_old - m_new)
        p = jnp.exp2((st - m_new).astype(BF16))
        acc_ref[comp] = alpha * acc_ref[comp] + jnp.dot(vt, p, preferred_element_type=F32)
        m_ref[comp] = m_new

    def step(k, vt, k_next):
        st1 = scores(k, 1)
        softmax_pv(0, st0_ref[:k.shape[0], :], vt)
        if k_next is not None:
            st0_ref[:k_next.shape[0], :] = scores(k_next, 0)
        softmax_pv(1, st1, vt)

    @pl.when(j < nk - 1)
    def _():
        step(kl_ref[...], vtl_ref[...], kn_ref[...])

    @pl.when(j == nk - 1)
    def _():
        step(kl_ref[...], vtl_ref[...], kc_ref[...])

    @pl.when(j == nk)
    def _():
        step(kc_ref[...], vtc_ref[...], None)
        lam = lam_ref[0]
        n = 2 * d
        o0 = acc_ref[0, :n, :] / acc_ref[0, n:n + 1, :]
        o1 = acc_ref[1, :n, :] / acc_ref[1, n:n + 1, :]
        ot = o0 - lam * o1
        yt = ot * lax.rsqrt(jnp.mean(ot * ot, axis=0, keepdims=True) + EPS)
        o_ref[...] = (yt.T * g_ref[...] * out_scale).astype(o_ref.dtype)


def diff_attention(qkv, vt, lam, subln_g, lambda_init, *, batch, seq, ctx_len, tq=1024, tk=2048):
    h = DIFF_HEADS
    nq = seq // tq
    nk = seq // tk
    vrows = 2 * DIFF_HEAD_DIM + ONES_ROWS
    ctx_blk0 = (batch * seq) // ctx_len
    body = functools.partial(_diff_body, nk=nk, out_scale=1.0 - lambda_init)
    assert nk >= 2 and ctx_len <= tk
    lat = lambda b, hh, i, j: b * nk + jnp.minimum(j, nk - 1)
    nxt = lambda b, hh, i, j: b * nk + jnp.minimum(j + 1, nk - 1)
    return pl.pallas_call(
        body, grid=(batch, h, nq, nk + 1),
        in_specs=[pl.BlockSpec(memory_space=pltpu.SMEM),
                  pl.BlockSpec((tq, LANES), lambda b, hh, i, j: (b * nq + i, hh)),
                  pl.BlockSpec((tk, LANES), lambda b, hh, i, j: (lat(b, hh, i, j), h + hh)),
                  pl.BlockSpec((tk, LANES), lambda b, hh, i, j: (nxt(b, hh, i, j), h + hh)),
                  pl.BlockSpec((vrows, tk), lambda b, hh, i, j: (hh, lat(b, hh, i, j))),
                  pl.BlockSpec((ctx_len, LANES), lambda b, hh, i, j: (ctx_blk0 + b, h + hh)),
                  pl.BlockSpec((vrows, ctx_len), lambda b, hh, i, j: (hh, ctx_blk0 + b)),
                  pl.BlockSpec((1, LANES), lambda b, hh, i, j: (0, 0))],
        out_specs=pl.BlockSpec((tq, LANES), lambda b, hh, i, j: (b * nq + i, hh)),
        out_shape=jax.ShapeDtypeStruct((batch * seq, h * LANES), BF16),
        scratch_shapes=[pltpu.VMEM((2, tq, LANES), BF16), pltpu.VMEM((2, 1, tq), F32),
                        pltpu.VMEM((2, vrows, tq), F32), pltpu.VMEM((tk, tq), F32)],
        compiler_params=_cparams(("parallel", "parallel", "parallel", "arbitrary")),
        name="diff_attention",
    )(lam, qkv, qkv, qkv, vt, qkv, vt, subln_g)


def _with_ones_rows(vt, heads):
    hw, rows = vt.shape
    blocks = vt.reshape(heads, hw // heads, rows)
    ones = jnp.ones((heads, ONES_ROWS, rows), vt.dtype)
    return jnp.concatenate([blocks, ones], axis=1).reshape(hw + heads * ONES_ROWS, rows)


def _extract16(s, pos):
    big = float(2 ** 20)
    rank = jnp.full(s.shape, float(PEER_TOPK), F32)
    vals = []
    for r in range(PEER_TOPK):
        m = jnp.max(s, axis=0, keepdims=True)
        sel = jnp.min(jnp.where(s == m, pos, big), axis=0, keepdims=True)
        hit = pos == sel
        vals.append(m)
        rank = jnp.where(hit, float(r), rank)
        s = jnp.where(hit, -jnp.inf, s)
    return vals, rank, rank < float(PEER_TOPK)


_CAND_GROUPS = ((0, 16), (1, 8), (2, 5), (3, 4), (4, 3), (5, 2), (6, 2), (7, 2))


def _peer_topk_body(q_ref, keys_ref, r1_ref, e1_ref, n0_ref, c0_ref):
    q = q_ref[...]
    kd = q.shape[1] // 2
    tb = q.shape[0]
    res = []
    for half in range(2):
        s = lax.dot_general(keys_ref[0, half], q[:, half * kd:(half + 1) * kd],
                            (((1,), (1,)), ((), ())), preferred_element_type=F32)
        pos = lax.broadcasted_iota(jnp.int32, s.shape, 0).astype(F32)
        vals, rank, _ = _extract16(s, pos)
        res.append((s, vals, rank))
    (s0, v0, rank0), (s1, v1, rank1) = res
    v1_lo = jnp.concatenate(v1[:SUBLANES], axis=0)
    v1_hi = jnp.concatenate(v1[SUBLANES:], axis=0)
    row8 = lax.broadcasted_iota(jnp.int32, (SUBLANES, tb), 0)
    pieces, poses = [], []
    for a, nb in _CAND_GROUPS:
        if nb > SUBLANES:
            pieces += [v0[a] + v1_lo, v0[a] + v1_hi]
            poses += [a * PEER_TOPK + row8, a * PEER_TOPK + SUBLANES + row8]
        else:
            pieces.append(jnp.where(row8 < nb, v0[a] + v1_lo, -jnp.inf))
            poses.append(a * PEER_TOPK + row8)
    pieces.append(jnp.concatenate(v0[SUBLANES:], axis=0) + v1[0])
    poses.append((row8 + SUBLANES) * PEER_TOPK)
    cand = jnp.concatenate(pieces, axis=0)
    cpos = jnp.concatenate(poses, axis=0).astype(F32)
    best, _, taken = _extract16(cand, cpos)
    z = sum(jnp.exp(b - best[0]) for b in best)
    takenf = taken.astype(F32)
    counts = []
    row = 0
    for a, nb in _CAND_GROUPS:
        rows = 2 * SUBLANES if nb > SUBLANES else SUBLANES
        counts.append(jnp.sum(takenf[row:row + rows], axis=0, keepdims=True))
        row += rows
    counts += [takenf[row + i:row + i + 1] for i in range(SUBLANES)]
    n0 = jnp.zeros(s0.shape, F32)
    for a in range(PEER_TOPK):
        n0 = jnp.where(rank0 == float(a), counts[a], n0)
    r1_ref[0] = rank1.astype(r1_ref.dtype)
    e1_ref[0] = jnp.exp(s1 - v1[0]).astype(e1_ref.dtype)
    n0_ref[0] = n0
    c0_ref[0] = jnp.exp(s0 - v0[0]) / z


def peer_topk(q, sub_keys, *, tb=1024):
    t = q.shape[0]
    h = PEER_HEADS
    spec = pl.BlockSpec((1, PEER_N_KEYS, tb), lambda i, hh: (hh, 0, i))
    shapes = [jax.ShapeDtypeStruct((h, PEER_N_KEYS, t), dt) for dt in (BF16, BF16, F32, F32)]
    return pl.pallas_call(
        _peer_topk_body, grid=(t // tb, h),
        in_specs=[pl.BlockSpec((tb, 2 * LANES), lambda i, hh: (i, hh)),
                  pl.BlockSpec((1, 2, PEER_N_KEYS, LANES), lambda i, hh: (hh, 0, 0, 0))],
        out_specs=[spec] * 4, out_shape=shapes,
        compiler_params=_cparams(("parallel", "parallel")), name="peer_topk",
    )(q, sub_keys)


PEER_PIPE_LAG = 2
GELU_C1 = math.sqrt(2.0 / math.pi)
GELU_C2 = 0.044715 * GELU_C1


def _peer_mlp_body(ht_ref, u_ref, vt_ref, r1_ref, e1_ref, n0_ref, c0_ref, x_ref, gate_ref, o_ref,
                   at_ref, wt_ref, acc_ref, *, blocks_per_step, n_chunks, n_flat):
    s = pl.program_id(0)
    cur = s % 2
    prev = 1 - cur
    c1 = jnp.clip(s - 1, 0, n_flat - 1) % n_chunks
    c2 = jnp.clip(s - 2, 0, n_flat - 1) % n_chunks

    @pl.when(s == 0)
    def _():
        at_ref[...] = jnp.zeros(at_ref.shape, F32)
        wt_ref[...] = jnp.zeros(wt_ref.shape, BF16)

    @pl.when(c2 == 0)
    def _():
        acc_ref[...] = jnp.zeros(acc_ref.shape, F32)

    acc_ref[...] += jnp.dot(vt_ref[...], wt_ref[prev], preferred_element_type=F32)

    tiles = []
    for ib in range(blocks_per_step):
        i = c1 * blocks_per_step + ib
        g = None
        for h in range(PEER_HEADS):
            tile_rows = (2 * SUBLANES, n0_ref.shape[2])
            reps = PEER_N_KEYS // (2 * SUBLANES)
            n0 = jnp.broadcast_to(n0_ref[h, pl.ds(i, 1), :], tile_rows).astype(BF16)
            c0 = jnp.broadcast_to(c0_ref[h, pl.ds(i, 1), :], tile_rows).astype(BF16)
            n0 = jnp.concatenate([n0] * reps, axis=0)
            c0 = jnp.concatenate([c0] * reps, axis=0)
            term = jnp.where(r1_ref[h] < n0, e1_ref[h], jnp.zeros((), BF16)) * c0
            g = term if g is None else g + term
        tiles.append(g)
    gt = jnp.concatenate(tiles, axis=0)
    a = at_ref[prev].astype(BF16)
    inner = a * (GELU_C1 + GELU_C2 * (a * a))
    half = 0.5 * a
    wt_ref[cur] = (half + half * jnp.tanh(inner)) * gt

    at_ref[cur] = jnp.dot(u_ref[...], ht_ref[...], preferred_element_type=F32)

    @pl.when((c2 == n_chunks - 1) & (s >= PEER_PIPE_LAG))
    def _():
        o_ref[...] = x_ref[...] + gate_ref[0] * acc_ref[...].T


def peer_experts(x, ht, routing, u, vt, g_f, *, tm=512, te=1024, seg_rows=8192):
    t, d = x.shape
    n_exp = u.shape[0]
    n_seg = g_f.shape[0]
    n_chunks = n_exp // te
    n_flat = (t // tm) * n_chunks
    seg = functools.partial(_seg_of_block, tm=tm, seg_rows=seg_rows, n_seg=n_seg)

    def lagged(lag):
        f = lambda s: jnp.clip(s - lag, 0, n_flat - 1)
        return (lambda s: f(s) // n_chunks), (lambda s: f(s) % n_chunks)

    (blk0, chunk0), (blk1, _), (blk2, chunk2) = lagged(0), lagged(1), lagged(2)
    route_spec = pl.BlockSpec((PEER_HEADS, PEER_N_KEYS, tm), lambda s: (0, 0, blk1(s)))
    body = functools.partial(_peer_mlp_body, blocks_per_step=te // PEER_N_KEYS, n_chunks=n_chunks, n_flat=n_flat)
    return pl.pallas_call(
        body, grid=(n_flat + PEER_PIPE_LAG,),
        in_specs=[pl.BlockSpec((d, tm), lambda s: (0, blk0(s))),
                  pl.BlockSpec((te, d), lambda s: (chunk0(s), 0)),
                  pl.BlockSpec((d, te), lambda s: (0, chunk2(s))),
                  route_spec, route_spec, route_spec, route_spec,
                  pl.BlockSpec((tm, d), lambda s: (blk2(s), 0)),
                  pl.BlockSpec((1, 1, d), lambda s: (seg(blk2(s)), 0, 0))],
        out_specs=pl.BlockSpec((tm, d), lambda s: (blk2(s), 0)),
        out_shape=jax.ShapeDtypeStruct((t, d), F32),
        scratch_shapes=[pltpu.VMEM((2, te, tm), F32), pltpu.VMEM((2, te, tm), BF16), pltpu.VMEM((d, tm), F32)],
        compiler_params=_cparams(("arbitrary",)), name="peer_experts",
    )(ht, u, vt, *routing, x, g_f)


def _rms_body(x_ref, g_ref, o_ref):
    xf = x_ref[...]
    o_ref[...] = xf * lax.rsqrt(jnp.mean(xf * xf, axis=-1, keepdims=True) + EPS) * g_ref[...]


def final_rms_norm(x, g, *, tm=1024):
    m, d = x.shape
    return pl.pallas_call(
        _rms_body, grid=(m // tm,),
        in_specs=[pl.BlockSpec((tm, d), lambda i: (i, 0)), pl.BlockSpec((1, d), lambda i: (0, 0))],
        out_specs=pl.BlockSpec((tm, d), lambda i: (i, 0)),
        out_shape=jax.ShapeDtypeStruct((m, d), F32),
        compiler_params=_cparams(("parallel",)), name="final_rms_norm")(x, g)


def _rope_tables(seq, extra_rows):
    quarter = DIFF_HEAD_DIM // 4
    t = jnp.arange(seq)
    row = (t // GRID_W).astype(F32)
    col = (t % GRID_W).astype(F32)
    inv = ROPE_BASE ** (-jnp.arange(quarter, dtype=F32) / quarter)
    cos_r, sin_r = jnp.cos(row[:, None] * inv), jnp.sin(row[:, None] * inv)
    cos_c, sin_c = jnp.cos(col[:, None] * inv), jnp.sin(col[:, None] * inv)
    cos64 = jnp.concatenate([cos_r, cos_r, cos_c, cos_c], axis=1)
    sin64 = jnp.concatenate([-sin_r, sin_r, -sin_c, sin_c], axis=1)
    cos = jnp.concatenate([cos64, cos64], axis=1)
    sin = jnp.concatenate([sin64, sin64], axis=1)
    cos = jnp.concatenate([cos, jnp.ones((extra_rows, LANES), F32)], axis=0)
    sin = jnp.concatenate([sin, jnp.zeros((extra_rows, LANES), F32)], axis=0)
    return cos, sin


def kernel(x, c, ctx, c_ctx, w_mod, b_mod, norm_mix_g, norm_ffn_g, even_w_in, even_w_out, na_rpb,
           odd_w_in, odd_w_out, diff_lambda_q1, diff_lambda_k1, diff_lambda_q2, diff_lambda_k2,
           diff_subln_g, peer_w_q, peer_sub_keys, peer_u, peer_v, final_norm_g):
    batch, seq, d = x.shape
    ctx_len = ctx.shape[1]
    depth = w_mod.shape[0]
    n_lat = batch * seq
    tm = 512
    xs = jnp.concatenate([x.reshape(n_lat, d), ctx.reshape(batch * ctx_len, d)], axis=0)
    cond = jnp.concatenate([c, c_ctx[None], jnp.zeros((SUBLANES - batch - 1, d), F32)], axis=0)
    cond = jax.nn.silu(cond)
    n_seg = batch + 1

    for layer in range(depth):
        last = layer == depth - 1
        jl = layer // 2
        mod = fused_matmul(cond, w_mod[layer].astype(BF16), tm=SUBLANES, tn=1024)[:n_seg] + b_mod[layer]
        sh_m, sc_m, g_m, sh_f, sc_f, g_f = [m_[:, None, :] for m_ in jnp.split(mod, 6, axis=-1)]
        norm_mix = (norm_mix_g[layer][None], sh_m, sc_m)
        norm_ffn = (norm_ffn_g[layer][None], sh_f, sc_f)
        if layer % 2 == 0:
            w_in = even_w_in[jl].astype(BF16)
            w_out = even_w_out[jl].astype(BF16)
            qkv = fused_matmul(xs, w_in[:, :3 * NA_WIDTH], norm=norm_mix, out_dtype=BF16, tm=tm)
            f = fused_matmul(xs, w_in[:, 3 * NA_WIDTH:], norm=norm_mix, tm=tm)
            attn = neighbourhood_attention(qkv, na_rpb[jl], batch=batch, seq=seq, ctx_len=ctx_len)
            four = fourier_long(f, batch=batch, seq=seq)
            if not last:
                attn = jnp.concatenate([attn, context_attention(qkv, batch=batch, seq=seq, ctx_len=ctx_len)], axis=0)
                four = jnp.concatenate([four, fourier_short(f, batch=batch, n=ctx_len, row_block0=n_lat // ctx_len)], axis=0)
            rows = attn.shape[0]
            xs_mix = fused_matmul(attn, w_out[:NA_WIDTH], res=(xs[:rows], g_m), tm=tm)
            xs_mix = fused_matmul(four, w_out[NA_WIDTH:], res=(xs_mix, g_m), tm=tm)
        else:
            lambda_init = 0.8 - 0.6 * math.exp(-0.3 * layer)
            lam = (jnp.exp(jnp.sum(diff_lambda_q1[jl] * diff_lambda_k1[jl]))
                   - jnp.exp(jnp.sum(diff_lambda_q2[jl] * diff_lambda_k2[jl])) + lambda_init).reshape(1)
            cos, sin = _rope_tables(seq, tm)
            v_width = DIFF_HEADS * 2 * DIFF_HEAD_DIM
            qkv, vt = fused_matmul(xs, odd_w_in[jl].astype(BF16), norm=norm_mix, out_dtype=BF16, tm=tm,
                                   rope=(cos, sin, 2 * DIFF_HEADS * 2 * DIFF_HEAD_DIM, seq // tm, n_lat // tm),
                                   t_cols=v_width)
            o = diff_attention(qkv, _with_ones_rows(vt, DIFF_HEADS), lam, diff_subln_g[jl][None], lambda_init,
                               batch=batch, seq=seq, ctx_len=ctx_len)
            if not last:
                raise NotImplementedError("context output of a differential-attention layer")
            xs_mix = fused_matmul(o, odd_w_out[jl].astype(BF16), res=(xs, g_m), tm=tm)
        q, ht = fused_matmul(xs_mix, peer_w_q[layer].astype(BF16), norm=norm_ffn, emit_ht=True, tm=tm)
        routing = peer_topk(q, peer_sub_keys[layer])
        xs = peer_experts(xs_mix, ht, routing, peer_u[layer].astype(BF16), peer_v[layer].astype(BF16).T, g_f)
        if not last and xs.shape[0] == n_lat:
            raise NotImplementedError("context stream dropped before the last layer")
    out = final_rms_norm(xs[:n_lat], final_norm_g[None])
    return out.reshape(batch, seq, d)
```
